```python
import math
import jax, jax.numpy as jnp
from jax import lax
import numpy as np

D_MODEL = 1024
BATCH = 4
SEQ = 8192
DEPTH = 1

C_CONV = 512
CONV_WIDTH = 31
N_HEADS = 4
HEAD_DIM = 64
V_DIM = 2 * HEAD_DIM
ATT_WIDTH = N_HEADS * V_DIM
Q_BLOCK = 128
D_FF = 4 * D_MODEL
PLE_DIM = 256
LN_EPS = 1e-5
DEEPNORM_ALPHA = (2.0 * DEPTH) ** 0.25
DEEPNORM_BETA = (8.0 * DEPTH) ** -0.25
N_GLU = 2 * C_CONV
N_QK = N_HEADS * 2 * HEAD_DIM
N_V = ATT_WIDTH
N_GATE = 2 * D_MODEL
N_IN = N_GLU + 2 * N_QK + N_V + N_GATE
SPLITS = (N_GLU, N_GLU + N_QK, N_GLU + 2 * N_QK, N_GLU + 2 * N_QK + N_V, N_GLU + 2 * N_QK + N_V + D_MODEL)
NEG_INF = -1e30

kernel_name = "hybrid_conformer_conv_diff_attn_deepnorm"


def layer_norm(x, g, b):
    xf = x.astype(jnp.float32)
    mu = jnp.mean(xf, axis=-1, keepdims=True)
    var = jnp.mean(jnp.square(xf - mu), axis=-1, keepdims=True)
    y = (xf - mu) * lax.rsqrt(var + LN_EPS)
    return (y * g.astype(jnp.float32) + b.astype(jnp.float32)).astype(x.dtype)


def rms_norm(x, g):
    xf = x.astype(jnp.float32)
    y = xf * lax.rsqrt(jnp.mean(jnp.square(xf), axis=-1, keepdims=True) + LN_EPS)
    return (y * g.astype(jnp.float32)).astype(x.dtype)


def alibi_slopes():
    return 2.0 ** (-8.0 * jnp.arange(1, N_HEADS + 1, dtype=jnp.float32) / N_HEADS)


def conformer_conv(z, w_dw, ln_g, ln_b, w_pw):
    a, g = jnp.split(z, 2, axis=-1)
    u = a * jax.nn.sigmoid(g)
    u = lax.conv_general_dilated(
        u, w_dw[:, None, :].astype(u.dtype), window_strides=(1,),
        padding=[(CONV_WIDTH - 1, 0)],
        dimension_numbers=("NWC", "WIO", "NWC"),
        feature_group_count=C_CONV)
    u = jax.nn.silu(layer_norm(u, ln_g, ln_b))
    return u @ w_pw


def diff_attention(q, k, v, lam):
    B, S = q.shape[0], q.shape[1]
    nb = S // Q_BLOCK
    scale = HEAD_DIM ** -0.5
    qb = (q * scale).reshape(B, nb, Q_BLOCK, N_HEADS, 2, HEAD_DIM).transpose(1, 0, 3, 4, 2, 5)
    kt = k.transpose(0, 2, 3, 1, 4)
    vt = v.transpose(0, 2, 1, 3)
    slopes = alibi_slopes()
    k_pos = jnp.arange(S)

    def block(args):
        q_blk, blk = args
        q_pos = blk * Q_BLOCK + jnp.arange(Q_BLOCK)
        dist = q_pos[:, None] - k_pos[None, :]
        bias = -slopes[:, None, None] * dist.astype(jnp.float32)
        s = jnp.einsum("bhcqd,bhcsd->bhcqs", q_blk, kt).astype(jnp.float32) + bias[None, :, None]
        s = jnp.where(dist >= 0, s, NEG_INF)
        pr = jax.nn.softmax(s, axis=-1)
        a = pr[:, :, 0] - lam * pr[:, :, 1]
        return jnp.einsum("bhqs,bhsv->bhqv", a.astype(vt.dtype), vt)

    o = lax.map(block, (qb, jnp.arange(nb)))
    return o.transpose(1, 0, 3, 2, 4).reshape(B, S, N_HEADS, V_DIM)


def setup_inputs(seed: int = 0) -> dict:
    key = jax.random.key(seed)
    ks = jax.random.split(key, 32)
    f32 = jnp.float32
    nrm = lambda k, shape, s: jax.random.normal(k, shape, f32) * s
    d_s = D_MODEL ** -0.5
    w_in = jnp.concatenate([
        nrm(ks[2], (DEPTH, D_MODEL, N_GLU + 2 * N_QK), d_s),
        nrm(ks[3], (DEPTH, D_MODEL, N_V), d_s * DEEPNORM_BETA),
        nrm(ks[4], (DEPTH, D_MODEL, N_GATE), d_s),
    ], axis=-1)
    return {
        "x": jax.random.normal(ks[0], (BATCH, SEQ, D_MODEL), f32),
        "p": jax.random.normal(ks[1], (DEPTH, BATCH, SEQ, PLE_DIM), f32),
        "ln0_g": 1.0 + nrm(ks[5], (D_MODEL,), 0.02),
        "ln0_b": nrm(ks[6], (D_MODEL,), 0.02),
        "w_in": w_in,
        "conv_w": nrm(ks[7], (DEPTH, CONV_WIDTH, C_CONV), CONV_WIDTH ** -0.5),
        "conv_ln_g": 1.0 + nrm(ks[8], (DEPTH, C_CONV), 0.02),
        "conv_ln_b": nrm(ks[9], (DEPTH, C_CONV), 0.02),
        "w_conv_out": nrm(ks[10], (DEPTH, C_CONV, D_MODEL), C_CONV ** -0.5 * DEEPNORM_BETA),
        "lambda_q1": nrm(ks[11], (DEPTH, HEAD_DIM), 0.1),
        "lambda_k1": nrm(ks[12], (DEPTH, HEAD_DIM), 0.1),
        "lambda_q2": nrm(ks[13], (DEPTH, HEAD_DIM), 0.1),
        "lambda_k2": nrm(ks[14], (DEPTH, HEAD_DIM), 0.1),
        "subln_g": 1.0 + nrm(ks[15], (DEPTH, ATT_WIDTH), 0.02),
        "w_attn_out": nrm(ks[16], (DEPTH, ATT_WIDTH, D_MODEL), ATT_WIDTH ** -0.5 * DEEPNORM_BETA),
        "w_o": nrm(ks[17], (DEPTH, D_MODEL, D_MODEL), d_s * DEEPNORM_BETA),
        "ln1_g": 1.0 + nrm(ks[18], (DEPTH, D_MODEL), 0.02),
        "ln1_b": nrm(ks[19], (DEPTH, D_MODEL), 0.02),
        "w_ff1": nrm(ks[20], (DEPTH, D_MODEL, D_FF), d_s * DEEPNORM_BETA),
        "w_ff2": nrm(ks[21], (DEPTH, D_FF, D_MODEL), D_FF ** -0.5 * DEEPNORM_BETA),
        "w_ple": nrm(ks[22], (DEPTH, PLE_DIM, D_MODEL), PLE_DIM ** -0.5 * DEEPNORM_BETA),
        "w_ple_gate": nrm(ks[23], (DEPTH, D_MODEL, D_MODEL), d_s),
        "ln2_g": 1.0 + nrm(ks[24], (DEPTH, D_MODEL), 0.02),
        "ln2_b": nrm(ks[25], (DEPTH, D_MODEL), 0.02),
    }


def reference(x, p, ln0_g, ln0_b, w_in, conv_w, conv_ln_g, conv_ln_b, w_conv_out,
              lambda_q1, lambda_k1, lambda_q2, lambda_k2, subln_g, w_attn_out, w_o,
              ln1_g, ln1_b, w_ff1, w_ff2, w_ple, w_ple_gate, ln2_g, ln2_b):
    B, S = x.shape[0], x.shape[1]
    h = layer_norm(x, ln0_g, ln0_b)
    for i in range(DEPTH):
        lambda_init = 0.8 - 0.6 * math.exp(-0.3 * i)
        z = h @ w_in[i]
        z_glu, z_q, z_k, z_v, g_conv, g_attn = jnp.split(z, SPLITS, axis=-1)
        y_conv = conformer_conv(z_glu, conv_w[i], conv_ln_g[i], conv_ln_b[i], w_conv_out[i])
        lam = (jnp.exp(jnp.sum(lambda_q1[i].astype(jnp.float32) * lambda_k1[i].astype(jnp.float32)))
               - jnp.exp(jnp.sum(lambda_q2[i].astype(jnp.float32) * lambda_k2[i].astype(jnp.float32)))
               + lambda_init)
        q = z_q.reshape(B, S, N_HEADS, 2, HEAD_DIM)
        k = z_k.reshape(B, S, N_HEADS, 2, HEAD_DIM)
        v = z_v.reshape(B, S, N_HEADS, V_DIM)
        o = diff_attention(q, k, v, lam)
        o = rms_norm(o, subln_g[i].reshape(N_HEADS, V_DIM)) * (1.0 - lambda_init)
        y_attn = o.reshape(B, S, ATT_WIDTH) @ w_attn_out[i]
        merged = jax.nn.sigmoid(g_conv) * y_conv + jax.nn.sigmoid(g_attn) * y_attn
        h = layer_norm(DEEPNORM_ALPHA * h + merged @ w_o[i], ln1_g[i], ln1_b[i])
        ff = jnp.square(jax.nn.relu(h @ w_ff1[i])) @ w_ff2[i]
        ple = jax.nn.sigmoid(h @ w_ple_gate[i]) * (p[i] @ w_ple[i])
        h = layer_norm(DEEPNORM_ALPHA * h + ff + ple, ln2_g[i], ln2_b[i])
    return h
```

```python
import functools
import math

import jax
import jax.numpy as jnp
from jax import lax
from jax.experimental import pallas as pl
from jax.experimental.pallas import tpu as pltpu

F32 = jnp.float32
BF16 = jnp.bfloat16

LN_EPS = 1e-5
NEG_INF = -1e30
N_HEADS = 4
HEAD_DIM = 64
V_DIM = 2 * HEAD_DIM
CONV_WIDTH = 31
CONV_HALO = 32

V7X_VMEM_BYTES = 64 * 1024 * 1024
VMEM_LIMIT_BYTES = V7X_VMEM_BYTES * 3 // 4

TM_PROJ = 512
T_ATT = 256
CONV_ROWS = 32
NT_DIMS = (((1,), (1,)), ((), ()))


def _layer_norm(x, g, b):
    mu = jnp.mean(x, axis=-1, keepdims=True)
    xc = x - mu
    var = jnp.mean(xc * xc, axis=-1, keepdims=True)
    return xc * lax.rsqrt(var + LN_EPS) * g + b


def _sigmoid(x):
    return 1.0 / (1.0 + jnp.exp(-x))


def _dot(a, b):
    return jnp.dot(a, b, preferred_element_type=F32)


def _in_proj_kernel(x_ref, g_ref, b_ref, wglu_ref, wqT_ref, wk_ref, wvT_ref,
                    h_ref, u_ref, qT_ref, k_ref, vT_ref, *, apply_ln, q_scale):
    x = x_ref[0]
    h = _layer_norm(x, g_ref[...], b_ref[...]) if apply_ln else x
    h_ref[0] = h
    hb = h.astype(BF16)
    c = u_ref.shape[-1]
    zg = _dot(hb, wglu_ref[...])
    u_ref[0] = zg[:, :c] * _sigmoid(zg[:, c:])
    k_ref[0] = _dot(hb, wk_ref[...]).astype(BF16)
    qT = lax.dot_general(wqT_ref[...], hb, NT_DIMS, preferred_element_type=F32)
    qT_ref[0] = (qT * q_scale).astype(BF16)
    vT = lax.dot_general(wvT_ref[...], hb, NT_DIMS, preferred_element_type=F32)
    t = vT_ref.shape[-1]
    for s in range(vT_ref.shape[1]):
        vT_ref[0, s] = vT[:, s * t:(s + 1) * t].astype(BF16)


def _in_proj(x, g, b, w_glu, w_qT, w_k, w_vT, *, apply_ln):
    B, S, D = x.shape
    tm = TM_PROJ
    c2 = w_glu.shape[1]
    nq = w_qT.shape[0]
    nv = w_vT.shape[0]
    const = lambda shape: pl.BlockSpec(shape, lambda bi, i: (0,) * len(shape))
    return pl.pallas_call(
        functools.partial(_in_proj_kernel, apply_ln=apply_ln, q_scale=HEAD_DIM ** -0.5),
        grid=(B, S // tm),
        in_specs=[
            pl.BlockSpec((1, tm, D), lambda bi, i: (bi, i, 0)),
            const((1, D)), const((1, D)),
            const(w_glu.shape), const(w_qT.shape), const(w_k.shape), const(w_vT.shape),
        ],
        out_specs=[
            pl.BlockSpec((1, tm, D), lambda bi, i: (bi, i, 0)),
            pl.BlockSpec((1, tm, c2 // 2), lambda bi, i: (bi, i, 0)),
            pl.BlockSpec((1, nq, tm), lambda bi, i: (bi, 0, i)),
            pl.BlockSpec((1, tm, nq), lambda bi, i: (bi, i, 0)),
            pl.BlockSpec((1, tm // T_ATT, nv, T_ATT), lambda bi, i: (bi, i, 0, 0)),
        ],
        out_shape=[
            jax.ShapeDtypeStruct((B, S, D), F32),
            jax.ShapeDtypeStruct((B, S, c2 // 2), F32),
            jax.ShapeDtypeStruct((B, nq, S), BF16),
            jax.ShapeDtypeStruct((B, S, nq), BF16),
            jax.ShapeDtypeStruct((B, S // T_ATT, nv, T_ATT), BF16),
        ],
        compiler_params=pltpu.CompilerParams(
            dimension_semantics=("arbitrary", "arbitrary"), vmem_limit_bytes=VMEM_LIMIT_BYTES),
        name="in_proj",
    )(x, g, b, w_glu, w_qT, w_k, w_vT)


def _attn_kernel(lq1_ref, lk1_ref, lq2_ref, lk2_ref, g_ref, qT_ref, k_ref, vT_ref, o_ref,
                 m_sc, l_sc, acc_sc, *, slopes, lam_init):
    t = T_ATT
    h = pl.program_id(1)
    i = pl.program_id(2)
    slope = jnp.float32(slopes[-1])
    for hh in range(len(slopes) - 1):
        slope = jnp.where(h == hh, jnp.float32(slopes[hh]), slope)

    kk = lax.broadcasted_iota(jnp.int32, (t, t), 0)
    qq = lax.broadcasted_iota(jnp.int32, (t, t), 1)
    bias = slope * kk.astype(F32)

    q = qT_ref[0]
    zeros = jnp.zeros((HEAD_DIM, t), BF16)
    q_maps = (jnp.concatenate([q[:HEAD_DIM], zeros], axis=0),
              jnp.concatenate([zeros, q[HEAD_DIM:]], axis=0))

    m_sc[...] = jnp.full(m_sc.shape, NEG_INF, F32)
    l_sc[...] = jnp.zeros(l_sc.shape, F32)
    acc_sc[...] = jnp.zeros(acc_sc.shape, F32)

    def step(j, masked):
        kb = k_ref[0, pl.ds(pl.multiple_of(j * t, t), t), :]
        vTj = vT_ref[0, j]
        off = slope * (j * t).astype(F32)
        for c in range(2):
            s = _dot(kb, q_maps[c]) + bias
            if masked:
                s = jnp.where(kk <= qq, s, NEG_INF)
            m_old = m_sc[c]
            m_new = jnp.maximum(m_old, jnp.max(s, axis=0, keepdims=True) + off)
            alpha = jnp.exp(m_old - m_new)
            p = jnp.exp(s - (m_new - off))
            l_sc[c] = alpha * l_sc[c] + jnp.sum(p, axis=0, keepdims=True)
            acc_sc[c] = alpha * acc_sc[c] + _dot(vTj, p.astype(BF16))
            m_sc[c] = m_new

    def body(j, carry):
        step(j, False)
        return carry

    lax.fori_loop(0, i, body, 0)
    step(i, True)

    s1 = jnp.sum(lq1_ref[...] * lk1_ref[...], axis=-1, keepdims=True)
    s2 = jnp.sum(lq2_ref[...] * lk2_ref[...], axis=-1, keepdims=True)
    lam = jnp.exp(s1) - jnp.exp(s2) + lam_init
    oT = acc_sc[0] / l_sc[0] - lam * (acc_sc[1] / l_sc[1])
    o = oT.T
    g = g_ref[pl.ds(h, 1), :]
    y = o * lax.rsqrt(jnp.mean(o * o, axis=-1, keepdims=True) + LN_EPS) * g
    o_ref[0] = (y * (1.0 - lam_init)).astype(o_ref.dtype)


def _attention(lq1, lk1, lq2, lk2, subln_g, qT, k, vT, *, lam_init):
    B, nq, S = qT.shape
    H = N_HEADS
    t = T_ATT
    slopes = tuple(2.0 ** (-8.0 * i / H) for i in range(1, H + 1))
    lam_spec = pl.BlockSpec((1, HEAD_DIM), lambda b, h, i: (0, 0))
    return pl.pallas_call(
        functools.partial(_attn_kernel, slopes=slopes, lam_init=lam_init),
        grid=(B, H, S // t),
        in_specs=[
            lam_spec, lam_spec, lam_spec, lam_spec,
            pl.BlockSpec((H, V_DIM), lambda b, h, i: (0, 0)),
            pl.BlockSpec((1, 2 * HEAD_DIM, t), lambda b, h, i: (b, h, i)),
            pl.BlockSpec((1, S, 2 * HEAD_DIM), lambda b, h, i: (b, 0, h)),
            pl.BlockSpec((1, S // t, V_DIM, t), lambda b, h, i: (b, 0, h, 0)),
        ],
        out_specs=pl.BlockSpec((1, t, V_DIM), lambda b, h, i: (b, i, h)),
        out_shape=jax.ShapeDtypeStruct((B, S, H * V_DIM), BF16),
        scratch_shapes=[
            pltpu.VMEM((2, 1, t), F32),
            pltpu.VMEM((2, 1, t), F32),
            pltpu.VMEM((2, V_DIM, t), F32),
        ],
        compiler_params=pltpu.CompilerParams(
            dimension_semantics=("arbitrary", "arbitrary", "arbitrary"),
            vmem_limit_bytes=VMEM_LIMIT_BYTES),
        name="attn",
    )(lq1, lk1, lq2, lk2, subln_g, qT, k, vT)


def _mix_kernel(h_ref, u_ref, halo_ref, o_ref, cw_ref, cg_ref, cb_ref, wco_ref, wao_ref,
                wgate_ref, wo_ref, g1_ref, b1_ref, out_ref, win_sc, conv_sc, *, alpha):
    i = pl.program_id(1)
    tm = u_ref.shape[1]
    d = h_ref.shape[-1]

    halo = halo_ref[0]
    win_sc[0:CONV_HALO, :] = jnp.where(i == 0, jnp.zeros_like(halo), halo)
    win_sc[CONV_HALO:, :] = u_ref[0]
    first = CONV_HALO - (CONV_WIDTH - 1)
    for r in range(0, tm, CONV_ROWS):
        acc = win_sc[r + first:r + first + CONV_ROWS, :] * cw_ref[0:1, :]
        for w in range(1, CONV_WIDTH):
            acc = acc + win_sc[r + first + w:r + first + w + CONV_ROWS, :] * cw_ref[w:w + 1, :]
        conv_sc[r:r + CONV_ROWS, :] = acc
    yc = _layer_norm(conv_sc[...], cg_ref[...], cb_ref[...])
    yc = yc * _sigmoid(yc)
    y_conv = _dot(yc.astype(BF16), wco_ref[...])
    y_attn = _dot(o_ref[0], wao_ref[...])

    h = h_ref[0]
    gates = _dot(h.astype(BF16), wgate_ref[...])
    merged = _sigmoid(gates[:, :d]) * y_conv + _sigmoid(gates[:, d:]) * y_attn
    r1 = alpha * h + _dot(merged.astype(BF16), wo_ref[...])
    out_ref[0] = _layer_norm(r1, g1_ref[...], b1_ref[...])


def _mix(h, u, o, conv_w, conv_g, conv_b, w_co, w_ao, w_gate, w_o, g1, b1, *, alpha):
    B, S, D = h.shape
    C = u.shape[-1]
    tm = TM_PROJ
    halo_blocks = tm // CONV_HALO
    const = lambda shape: pl.BlockSpec(shape, lambda bi, i: (0,) * len(shape))
    return pl.pallas_call(
        functools.partial(_mix_kernel, alpha=alpha),
        grid=(B, S // tm),
        in_specs=[
            pl.BlockSpec((1, tm, D), lambda bi, i: (bi, i, 0)),
            pl.BlockSpec((1, tm, C), lambda bi, i: (bi, i, 0)),
            pl.BlockSpec((1, CONV_HALO, C), lambda bi, i: (bi, jnp.maximum(i * halo_blocks - 1, 0), 0)),
            pl.BlockSpec((1, tm, o.shape[-1]), lambda bi, i: (bi, i, 0)),
            const(conv_w.shape), const((1, C)), const((1, C)),
            const(w_co.shape), const(w_ao.shape), const(w_gate.shape), const(w_o.shape),
            const((1, D)), const((1, D)),
        ],
        out_specs=pl.BlockSpec((1, tm, D), lambda bi, i: (bi, i, 0)),
        out_shape=jax.ShapeDtypeStruct((B, S, D), F32),
        scratch_shapes=[
            pltpu.VMEM((CONV_HALO + tm, C), F32),
            pltpu.VMEM((tm, C), F32),
        ],
        compiler_params=pltpu.CompilerParams(
            dimension_semantics=("arbitrary", "arbitrary"), vmem_limit_bytes=VMEM_LIMIT_BYTES),
        name="mix",
    )(h, u, u, o, conv_w, conv_g, conv_b, w_co, w_ao, w_gate, w_o, g1, b1)


def _ffn_kernel(h_ref, p_ref, w1_ref, w2_ref, wpg_ref, wple_ref, g2_ref, b2_ref, out_ref,
                *, alpha, ff_chunk):
    h = h_ref[0]
    hb = h.astype(BF16)
    acc = alpha * h
    for c in range(0, w1_ref.shape[1], ff_chunk):
        a = jnp.maximum(_dot(hb, w1_ref[:, c:c + ff_chunk]), 0.0)
        acc = acc + _dot((a * a).astype(BF16), w2_ref[c:c + ff_chunk, :])
    ple = _dot(p_ref[0].astype(BF16), wple_ref[...])
    acc = acc + _sigmoid(_dot(hb, wpg_ref[...])) * ple
    out_ref[0] = _layer_norm(acc, g2_ref[...], b2_ref[...])


def _ffn(h, p, w1, w2, w_pg, w_ple, g2, b2, *, alpha):
    B, S, D = h.shape
    tm = TM_PROJ
    const = lambda shape: pl.BlockSpec(shape, lambda bi, i: (0,) * len(shape),
                                       pipeline_mode=pl.Buffered(1))
    return pl.pallas_call(
        functools.partial(_ffn_kernel, alpha=alpha, ff_chunk=D),
        grid=(B, S // tm),
        in_specs=[
            pl.BlockSpec((1, tm, D), lambda bi, i: (bi, i, 0)),
            pl.BlockSpec((1, tm, p.shape[-1]), lambda bi, i: (bi, i, 0)),
            const(w1.shape), const(w2.shape), const(w_pg.shape), const(w_ple.shape),
            const((1, D)), const((1, D)),
        ],
        out_specs=pl.BlockSpec((1, tm, D), lambda bi, i: (bi, i, 0)),
        out_shape=jax.ShapeDtypeStruct((B, S, D), F32),
        compiler_params=pltpu.CompilerParams(
            dimension_semantics=("arbitrary", "arbitrary"), vmem_limit_bytes=VMEM_LIMIT_BYTES),
        name="ffn",
    )(h, p, w1, w2, w_pg, w_ple, g2, b2)


def kernel(x, p, ln0_g, ln0_b, w_in, conv_w, conv_ln_g, conv_ln_b, w_conv_out, lambda_q1, lambda_k1, lambda_q2, lambda_k2, subln_g, w_attn_out, w_o, ln1_g, ln1_b, w_ff1, w_ff2, w_ple, w_ple_gate, ln2_g, ln2_b):
    depth = w_in.shape[0]
    d_model = x.shape[-1]
    c_conv = conv_w.shape[-1]
    n_glu = 2 * c_conv
    n_qk = N_HEADS * 2 * HEAD_DIM
    n_v = N_HEADS * V_DIM
    alpha = (2.0 * depth) ** 0.25
    row = lambda v: v.reshape(1, -1).astype(F32)

    h = x
    for i in range(depth):
        lam_init = 0.8 - 0.6 * math.exp(-0.3 * i)
        w = w_in[i].astype(BF16)
        w_glu = w[:, :n_glu]
        w_qT = w[:, n_glu:n_glu + n_qk].T
        w_k = w[:, n_glu + n_qk:n_glu + 2 * n_qk]
        w_vT = w[:, n_glu + 2 * n_qk:n_glu + 2 * n_qk + n_v].T
        w_gate = w[:, n_glu + 2 * n_qk + n_v:]

        h, u, qT, k, vT = _in_proj(h, row(ln0_g), row(ln0_b), w_glu, w_qT, w_k, w_vT,
                                   apply_ln=(i == 0))
        o = _attention(row(lambda_q1[i]), row(lambda_k1[i]), row(lambda_q2[i]), row(lambda_k2[i]),
                       subln_g[i].reshape(N_HEADS, V_DIM).astype(F32), qT, k, vT, lam_init=lam_init)
        h = _mix(h, u, o, conv_w[i].astype(F32), row(conv_ln_g[i]), row(conv_ln_b[i]),
                 w_conv_out[i].astype(BF16), w_attn_out[i].astype(BF16), w_gate,
                 w_o[i].astype(BF16), row(ln1_g[i]), row(ln1_b[i]), alpha=alpha)
        h = _ffn(h, p[i], w_ff1[i].astype(BF16), w_ff2[i].astype(BF16), w_ple_gate[i].astype(BF16),
                 w_ple[i].astype(BF16), row(ln2_g[i]), row(ln2_b[i]), alpha=alpha)
    return h
```

```python
import functools
import math

import jax
import jax.numpy as jnp
from jax import lax
from jax.experimental import pallas as pl
from jax.experimental.pallas import tpu as pltpu

F32 = jnp.float32
BF16 = jnp.bfloat16

LN_EPS = 1e-5
NEG_INF = -1e30
N_HEADS = 4
HEAD_DIM = 64
V_DIM = 2 * HEAD_DIM
CONV_WIDTH = 31
CONV_HALO = 32

V7X_VMEM_BYTES = 64 * 1024 * 1024
VMEM_LIMIT_BYTES = V7X_VMEM_BYTES * 3 // 4

TM_PROJ = 512
TQ_ATT = 512
TK_ATT = 512
V_CHUNK = TM_PROJ
CONV_ROWS = 32
SUBLANES = 8
NT_DIMS = (((1,), (1,)), ((), ()))
LOG2E = math.log2(math.e)
L_ROWS = 16


def _layer_norm(x, g, b):
    mu = jnp.mean(x, axis=-1, keepdims=True)
    xc = x - mu
    var = jnp.mean(xc * xc, axis=-1, keepdims=True)
    return xc * lax.rsqrt(var + LN_EPS) * g + b


def _sigmoid(x):
    return 1.0 / (1.0 + jnp.exp(-x))


def _dot(a, b):
    return jnp.dot(a, b, preferred_element_type=F32)


def _in_proj_kernel(x_ref, g_ref, b_ref, wglu_ref, wqT_ref, wk_ref, wvT_ref,
                    h_ref, u_ref, qT_ref, k_ref, vT_ref, *, apply_ln, q_scale):
    x = x_ref[0]
    h = _layer_norm(x, g_ref[...], b_ref[...]) if apply_ln else x
    h_ref[0] = h
    hb = h.astype(BF16)
    c = u_ref.shape[-1]
    zg = _dot(hb, wglu_ref[...])
    u_ref[0] = zg[:, :c] * _sigmoid(zg[:, c:])
    k_ref[0] = _dot(hb, wk_ref[...]).astype(BF16)
    qT = lax.dot_general(wqT_ref[...], hb, NT_DIMS, preferred_element_type=F32)
    qT_ref[0] = (qT * q_scale).astype(BF16)
    vT = lax.dot_general(wvT_ref[...], hb, NT_DIMS, preferred_element_type=F32)
    vT_ref[0, 0] = vT.astype(BF16)


def _in_proj(x, g, b, w_glu, w_qT, w_k, w_vT, *, apply_ln):
    B, S, D = x.shape
    tm = TM_PROJ
    c2 = w_glu.shape[1]
    nq = w_qT.shape[0]
    nv = w_vT.shape[0]
    const = lambda shape: pl.BlockSpec(shape, lambda bi, i: (0,) * len(shape))
    return pl.pallas_call(
        functools.partial(_in_proj_kernel, apply_ln=apply_ln, q_scale=HEAD_DIM ** -0.5 * LOG2E),
        grid=(B, S // tm),
        in_specs=[
            pl.BlockSpec((1, tm, D), lambda bi, i: (bi, i, 0)),
            const((1, D)), const((1, D)),
            const(w_glu.shape), const(w_qT.shape), const(w_k.shape), const(w_vT.shape),
        ],
        out_specs=[
            pl.BlockSpec((1, tm, D), lambda bi, i: (bi, i, 0)),
            pl.BlockSpec((1, tm, c2 // 2), lambda bi, i: (bi, i, 0)),
            pl.BlockSpec((1, nq, tm), lambda bi, i: (bi, 0, i)),
            pl.BlockSpec((1, tm, nq), lambda bi, i: (bi, i, 0)),
            pl.BlockSpec((1, 1, nv, V_CHUNK), lambda bi, i: (bi, i, 0, 0)),
        ],
        out_shape=[
            jax.ShapeDtypeStruct((B, S, D), F32),
            jax.ShapeDtypeStruct((B, S, c2 // 2), F32),
            jax.ShapeDtypeStruct((B, nq, S), BF16),
            jax.ShapeDtypeStruct((B, S, nq), BF16),
            jax.ShapeDtypeStruct((B, S // V_CHUNK, nv, V_CHUNK), BF16),
        ],
        compiler_params=pltpu.CompilerParams(
            dimension_semantics=("arbitrary", "arbitrary"), vmem_limit_bytes=VMEM_LIMIT_BYTES),
        name="in_proj",
    )(x, g, b, w_glu, w_qT, w_k, w_vT)


def _attn_kernel(lq1_ref, lk1_ref, lq2_ref, lk2_ref, g_ref, qT_ref, k_ref, vT_ref, o_ref,
                 s_sc, mcur_sc, m_sc, p_sc, alpha_sc, acc_sc, *, slopes, lam_init):
    tq, tk = TQ_ATT, TK_ATT
    n_diag = tq // tk
    v_chunks = tk // V_CHUNK
    h = pl.program_id(1)
    i = pl.program_id(2)
    n_full = i * n_diag
    slope = jnp.float32(slopes[-1] * LOG2E)
    for hh in range(len(slopes) - 1):
        slope = jnp.where(h == hh, jnp.float32(slopes[hh] * LOG2E), slope)

    kk = lax.broadcasted_iota(jnp.int32, (tk, 2 * tq), 0)
    qq = lax.broadcasted_iota(jnp.int32, (tk, 2 * tq), 1) & (tq - 1)
    bias = slope * kk.astype(F32)

    q = qT_ref[0]
    zeros = jnp.zeros((HEAD_DIM, tq), BF16)
    q_both = jnp.concatenate([jnp.concatenate([q[:HEAD_DIM], zeros], axis=0),
                              jnp.concatenate([zeros, q[HEAD_DIM:]], axis=0)], axis=1)
    ones = jnp.ones((L_ROWS, tk), BF16)

    def scores(j):
        kb = k_ref[0, pl.ds(pl.multiple_of(j * tk, tk), tk), :]
        s = _dot(kb, q_both) + bias
        s_sc[...] = s
        mcur_sc[...] = jnp.max(s, axis=0, keepdims=True)

    def softmax(j, diag):
        off = slope * (j * tk).astype(F32)
        if diag is None:
            s = s_sc[...]
            m_cur = mcur_sc[...]
        else:
            s = jnp.where(kk + diag * tk <= qq, s_sc[...], NEG_INF)
            m_cur = jnp.max(s, axis=0, keepdims=True)
        m_old = m_sc[...]
        m_new = jnp.maximum(m_old, m_cur + off)
        alpha_sc[...] = jnp.exp2(m_old - m_new)
        p_sc[...] = jnp.exp2(s - (m_new - off)).astype(BF16)
        m_sc[...] = m_new

    def accumulate(j):
        vT = [vT_ref[0, j * v_chunks + c] for c in range(v_chunks)]
        vT_aug = jnp.concatenate([jnp.concatenate(vT, axis=1), ones], axis=0)
        acc_sc[...] = alpha_sc[...] * acc_sc[...] + _dot(vT_aug, p_sc[...])

    m_sc[...] = jnp.full(m_sc.shape, NEG_INF, F32)
    p_sc[...] = jnp.zeros(p_sc.shape, BF16)
    alpha_sc[...] = jnp.ones(alpha_sc.shape, F32)
    acc_sc[...] = jnp.zeros(acc_sc.shape, F32)
    scores(0)

    def body(j, carry):
        accumulate(jnp.maximum(j - 1, 0))
        softmax(j, None)
        scores(j + 1)
        return carry

    lax.fori_loop(0, n_full, body, 0)
    for d in range(n_diag):
        accumulate(jnp.maximum(n_full + d - 1, 0))
        softmax(n_full + d, d)
        if d + 1 < n_diag:
            scores(n_full + d + 1)
    accumulate(n_full + n_diag - 1)

    s1 = jnp.sum(lq1_ref[...] * lk1_ref[...], axis=-1, keepdims=True)
    s2 = jnp.sum(lq2_ref[...] * lk2_ref[...], axis=-1, keepdims=True)
    lam = jnp.exp(s1) - jnp.exp(s2) + lam_init
    acc = acc_sc[...]
    oT = (acc[:V_DIM, :tq] / acc[V_DIM:V_DIM + 1, :tq]
          - lam * (acc[:V_DIM, tq:] / acc[V_DIM:V_DIM + 1, tq:]))
    o = oT.T
    g = g_ref[pl.ds(h, 1), :]
    y = o * lax.rsqrt(jnp.mean(o * o, axis=-1, keepdims=True) + LN_EPS) * g
    o_ref[0] = (y * (1.0 - lam_init)).astype(o_ref.dtype)


def _attention(lq1, lk1, lq2, lk2, subln_g, qT, k, vT, *, lam_init):
    B, nq, S = qT.shape
    H = N_HEADS
    tq, tk = TQ_ATT, TK_ATT
    slopes = tuple(2.0 ** (-8.0 * i / H) for i in range(1, H + 1))
    lam_spec = pl.BlockSpec((1, HEAD_DIM), lambda b, h, i: (0, 0))
    return pl.pallas_call(
        functools.partial(_attn_kernel, slopes=slopes, lam_init=lam_init),
        grid=(B, H, S // tq),
        in_specs=[
            lam_spec, lam_spec, lam_spec, lam_spec,
            pl.BlockSpec((H, V_DIM), lambda b, h, i: (0, 0)),
            pl.BlockSpec((1, 2 * HEAD_DIM, tq), lambda b, h, i: (b, h, i)),
            pl.BlockSpec((1, S, 2 * HEAD_DIM), lambda b, h, i: (b, 0, h)),
            pl.BlockSpec((1, S // V_CHUNK, V_DIM, V_CHUNK), lambda b, h, i: (b, 0, h, 0)),
        ],
        out_specs=pl.BlockSpec((1, tq, V_DIM), lambda b, h, i: (b, i, h)),
        out_shape=jax.ShapeDtypeStruct((B, S, H * V_DIM), BF16),
        scratch_shapes=[
            pltpu.VMEM((tk, 2 * tq), F32),
            pltpu.VMEM((1, 2 * tq), F32),
            pltpu.VMEM((1, 2 * tq), F32),
            pltpu.VMEM((tk, 2 * tq), BF16),
            pltpu.VMEM((1, 2 * tq), F32),
            pltpu.VMEM((V_DIM + L_ROWS, 2 * tq), F32),
        ],
        compiler_params=pltpu.CompilerParams(
            dimension_semantics=("arbitrary", "arbitrary", "arbitrary"),
            vmem_limit_bytes=VMEM_LIMIT_BYTES),
        name="attn",
    )(lq1, lk1, lq2, lk2, subln_g, qT, k, vT)


def _mix_kernel(h_ref, u_ref, halo_ref, o_ref, cw_ref, cg_ref, cb_ref, wco_ref, wao_ref,
                wgate_ref, wo_ref, g1_ref, b1_ref, out_ref, win_sc, conv_sc, *, alpha):
    i = pl.program_id(1)
    tm = u_ref.shape[1]
    d = h_ref.shape[-1]

    halo = halo_ref[0]
    win_sc[0, 0:CONV_HALO, :] = jnp.where(i == 0, jnp.zeros_like(halo), halo)
    win_sc[0, CONV_HALO:CONV_HALO + tm, :] = u_ref[0]
    for s in range(1, SUBLANES):
        win_sc[s, 0:CONV_HALO + tm - SUBLANES, :] = win_sc[0, s:s + CONV_HALO + tm - SUBLANES, :]
    first = CONV_HALO - (CONV_WIDTH - 1)
    for r in range(0, tm, CONV_ROWS):
        acc = None
        for w in range(CONV_WIDTH):
            a, s = divmod(first + w, SUBLANES)
            row = r + a * SUBLANES
            term = win_sc[s, row:row + CONV_ROWS, :] * cw_ref[w:w + 1, :]
            acc = term if acc is None else acc + term
        conv_sc[r:r + CONV_ROWS, :] = acc
    yc = _layer_norm(conv_sc[...], cg_ref[...], cb_ref[...])
    yc = yc * _sigmoid(yc)
    y_conv = _dot(yc.astype(BF16), wco_ref[...])
    y_attn = _dot(o_ref[0], wao_ref[...])

    h = h_ref[0]
    gates = _dot(h.astype(BF16), wgate_ref[...])
    merged = _sigmoid(gates[:, :d]) * y_conv + _sigmoid(gates[:, d:]) * y_attn
    r1 = alpha * h + _dot(merged.astype(BF16), wo_ref[...])
    out_ref[0] = _layer_norm(r1, g1_ref[...], b1_ref[...])


def _mix(h, u, o, conv_w, conv_g, conv_b, w_co, w_ao, w_gate, w_o, g1, b1, *, alpha):
    B, S, D = h.shape
    C = u.shape[-1]
    tm = TM_PROJ
    halo_blocks = tm // CONV_HALO
    const = lambda shape: pl.BlockSpec(shape, lambda bi, i: (0,) * len(shape))
    return pl.pallas_call(
        functools.partial(_mix_kernel, alpha=alpha),
        grid=(B, S // tm),
        in_specs=[
            pl.BlockSpec((1, tm, D), lambda bi, i: (bi, i, 0)),
            pl.BlockSpec((1, tm, C), lambda bi, i: (bi, i, 0)),
            pl.BlockSpec((1, CONV_HALO, C), lambda bi, i: (bi, jnp.maximum(i * halo_blocks - 1, 0), 0)),
            pl.BlockSpec((1, tm, o.shape[-1]), lambda bi, i: (bi, i, 0)),
            const(conv_w.shape), const((1, C)), const((1, C)),
            const(w_co.shape), const(w_ao.shape), const(w_gate.shape), const(w_o.shape),
            const((1, D)), const((1, D)),
        ],
        out_specs=pl.BlockSpec((1, tm, D), lambda bi, i: (bi, i, 0)),
        out_shape=jax.ShapeDtypeStruct((B, S, D), F32),
        scratch_shapes=[
            pltpu.VMEM((SUBLANES, CONV_HALO + tm, C), F32),
            pltpu.VMEM((tm, C), F32),
        ],
        compiler_params=pltpu.CompilerParams(
            dimension_semantics=("arbitrary", "arbitrary"), vmem_limit_bytes=VMEM_LIMIT_BYTES),
        name="mix",
    )(h, u, u, o, conv_w, conv_g, conv_b, w_co, w_ao, w_gate, w_o, g1, b1)


def _ffn_kernel(h_ref, p_ref, w1_ref, w2_ref, wpg_ref, wple_ref, g2_ref, b2_ref, out_ref,
                *, alpha, ff_chunk):
    h = h_ref[0]
    hb = h.astype(BF16)
    acc = alpha * h
    for c in range(0, w1_ref.shape[1], ff_chunk):
        a = jnp.maximum(_dot(hb, w1_ref[:, c:c + ff_chunk]), 0.0)
        acc = acc + _dot((a * a).astype(BF16), w2_ref[c:c + ff_chunk, :])
    ple = _dot(p_ref[0].astype(BF16), wple_ref[...])
    acc = acc + _sigmoid(_dot(hb, wpg_ref[...])) * ple
    out_ref[0] = _layer_norm(acc, g2_ref[...], b2_ref[...])


def _ffn(h, p, w1, w2, w_pg, w_ple, g2, b2, *, alpha):
    B, S, D = h.shape
    tm = TM_PROJ
    const = lambda shape: pl.BlockSpec(shape, lambda bi, i: (0,) * len(shape),
                                       pipeline_mode=pl.Buffered(1))
    return pl.pallas_call(
        functools.partial(_ffn_kernel, alpha=alpha, ff_chunk=D),
        grid=(B, S // tm),
        in_specs=[
            pl.BlockSpec((1, tm, D), lambda bi, i: (bi, i, 0)),
            pl.BlockSpec((1, tm, p.shape[-1]), lambda bi, i: (bi, i, 0)),
            const(w1.shape), const(w2.shape), const(w_pg.shape), const(w_ple.shape),
            const((1, D)), const((1, D)),
        ],
        out_specs=pl.BlockSpec((1, tm, D), lambda bi, i: (bi, i, 0)),
        out_shape=jax.ShapeDtypeStruct((B, S, D), F32),
        compiler_params=pltpu.CompilerParams(
            dimension_semantics=("arbitrary", "arbitrary"), vmem_limit_bytes=VMEM_LIMIT_BYTES),
        name="ffn",
    )(h, p, w1, w2, w_pg, w_ple, g2, b2)


def kernel(x, p, ln0_g, ln0_b, w_in, conv_w, conv_ln_g, conv_ln_b, w_conv_out, lambda_q1, lambda_k1, lambda_q2, lambda_k2, subln_g, w_attn_out, w_o, ln1_g, ln1_b, w_ff1, w_ff2, w_ple, w_ple_gate, ln2_g, ln2_b):
    depth = w_in.shape[0]
    c_conv = conv_w.shape[-1]
    n_glu = 2 * c_conv
    n_qk = N_HEADS * 2 * HEAD_DIM
    n_v = N_HEADS * V_DIM
    alpha = (2.0 * depth) ** 0.25
    row = lambda v: v.reshape(1, -1).astype(F32)

    h = x
    for i in range(depth):
        lam_init = 0.8 - 0.6 * math.exp(-0.3 * i)
        w = w_in[i].astype(BF16)
        w_glu = w[:, :n_glu]
        w_qT = w[:, n_glu:n_glu + n_qk].T
        w_k = w[:, n_glu + n_qk:n_glu + 2 * n_qk]
        w_vT = w[:, n_glu + 2 * n_qk:n_glu + 2 * n_qk + n_v].T
        w_gate = w[:, n_glu + 2 * n_qk + n_v:]

        h, u, qT, k, vT = _in_proj(h, row(ln0_g), row(ln0_b), w_glu, w_qT, w_k, w_vT,
                                   apply_ln=(i == 0))
        o = _attention(row(lambda_q1[i]), row(lambda_k1[i]), row(lambda_q2[i]), row(lambda_k2[i]),
                       subln_g[i].reshape(N_HEADS, V_DIM).astype(F32), qT, k, vT, lam_init=lam_init)
        h = _mix(h, u, o, conv_w[i].astype(F32), row(conv_ln_g[i]), row(conv_ln_b[i]),
                 w_conv_out[i].astype(BF16), w_attn_out[i].astype(BF16), w_gate,
                 w_o[i].astype(BF16), row(ln1_g[i]), row(ln1_b[i]), alpha=alpha)
        h = _ffn(h, p[i], w_ff1[i].astype(BF16), w_ff2[i].astype(BF16), w_ple_gate[i].astype(BF16),
                 w_ple[i].astype(BF16), row(ln2_g[i]), row(ln2_b[i]), alpha=alpha)
    return h
```

```python
import functools
import math

import jax
import jax.numpy as jnp
from jax import lax
from jax.experimental import pallas as pl
from jax.experimental.pallas import tpu as pltpu

F32 = jnp.float32
BF16 = jnp.bfloat16

LN_EPS = 1e-5
NEG_INF = -1e30
N_HEADS = 4
HEAD_DIM = 64
V_DIM = 2 * HEAD_DIM
CONV_WIDTH = 31
CONV_HALO = 32

V7X_VMEM_BYTES = 64 * 1024 * 1024
VMEM_LIMIT_BYTES = V7X_VMEM_BYTES * 3 // 4

TM_PROJ = 512
TQ_ATT = 1024
TK_ATT = 512
V_CHUNK = TM_PROJ
CONV_ROWS = 32
SUBLANES = 8
NT_DIMS = (((1,), (1,)), ((), ()))
LOG2E = math.log2(math.e)
L_ROWS = 16


def _layer_norm(x, g, b):
    mu = jnp.mean(x, axis=-1, keepdims=True)
    xc = x - mu
    var = jnp.mean(xc * xc, axis=-1, keepdims=True)
    return xc * lax.rsqrt(var + LN_EPS) * g + b


def _sigmoid(x):
    return 1.0 / (1.0 + jnp.exp(-x))


def _dot(a, b):
    return jnp.dot(a, b, preferred_element_type=F32)


def _in_proj_kernel(x_ref, g_ref, b_ref, wglu_ref, wqT_ref, wk_ref, wvT_ref,
                    h_ref, u_ref, qT_ref, k_ref, vT_ref, *, apply_ln, q_scale):
    x = x_ref[0]
    h = _layer_norm(x, g_ref[...], b_ref[...]) if apply_ln else x
    h_ref[0] = h
    hb = h.astype(BF16)
    c = u_ref.shape[-1]
    zg = _dot(hb, wglu_ref[...])
    u_ref[0] = zg[:, :c] * _sigmoid(zg[:, c:])
    k_ref[0] = _dot(hb, wk_ref[...]).astype(BF16)
    qT = lax.dot_general(wqT_ref[...], hb, NT_DIMS, preferred_element_type=F32)
    qT_ref[0, 0] = (qT * q_scale).astype(BF16)
    vT = lax.dot_general(wvT_ref[...], hb, NT_DIMS, preferred_element_type=F32)
    vT_ref[0, 0] = vT.astype(BF16)


def _in_proj(x, g, b, w_glu, w_qT, w_k, w_vT, *, apply_ln):
    B, S, D = x.shape
    tm = TM_PROJ
    c2 = w_glu.shape[1]
    nq = w_qT.shape[0]
    nv = w_vT.shape[0]
    const = lambda shape: pl.BlockSpec(shape, lambda bi, i: (0,) * len(shape))
    return pl.pallas_call(
        functools.partial(_in_proj_kernel, apply_ln=apply_ln, q_scale=HEAD_DIM ** -0.5 * LOG2E),
        grid=(B, S // tm),
        in_specs=[
            pl.BlockSpec((1, tm, D), lambda bi, i: (bi, i, 0)),
            const((1, D)), const((1, D)),
            const(w_glu.shape), const(w_qT.shape), const(w_k.shape), const(w_vT.shape),
        ],
        out_specs=[
            pl.BlockSpec((1, tm, D), lambda bi, i: (bi, i, 0)),
            pl.BlockSpec((1, tm, c2 // 2), lambda bi, i: (bi, i, 0)),
            pl.BlockSpec((1, 1, nq, V_CHUNK), lambda bi, i: (bi, i, 0, 0)),
            pl.BlockSpec((1, tm, nq), lambda bi, i: (bi, i, 0)),
            pl.BlockSpec((1, 1, nv, V_CHUNK), lambda bi, i: (bi, i, 0, 0)),
        ],
        out_shape=[
            jax.ShapeDtypeStruct((B, S, D), F32),
            jax.ShapeDtypeStruct((B, S, c2 // 2), F32),
            jax.ShapeDtypeStruct((B, S // V_CHUNK, nq, V_CHUNK), BF16),
            jax.ShapeDtypeStruct((B, S, nq), BF16),
            jax.ShapeDtypeStruct((B, S // V_CHUNK, nv, V_CHUNK), BF16),
        ],
        compiler_params=pltpu.CompilerParams(
            dimension_semantics=("arbitrary", "arbitrary"), vmem_limit_bytes=VMEM_LIMIT_BYTES),
        name="in_proj",
    )(x, g, b, w_glu, w_qT, w_k, w_vT)


def _attn_kernel(lq1_ref, lk1_ref, lq2_ref, lk2_ref, g_ref, qT_ref, k_ref, vT_ref, o_ref,
                 bm_sc, s_sc, mcur_sc, m_sc, p_sc, alpha_sc, acc_sc, *, slopes, lam_init, n_q):
    tq, tk = TQ_ATT, TK_ATT
    n_diag = tq // tk
    q_chunks = tq // V_CHUNK
    v_chunks = tk // V_CHUNK
    h = pl.program_id(0)
    slope = jnp.float32(slopes[-1] * LOG2E)
    for hh in range(len(slopes) - 1):
        slope = jnp.where(h == hh, jnp.float32(slopes[hh] * LOG2E), slope)

    @pl.when(pl.program_id(1) == 0)
    def _():
        kk = lax.broadcasted_iota(jnp.int32, (tk, 2 * tq), 0)
        qq = lax.broadcasted_iota(jnp.int32, (tk, 2 * tq), 1) & (tq - 1)
        bias = slope * kk.astype(F32)
        bm_sc[0] = bias
        for d in range(n_diag):
            bm_sc[1 + d] = jnp.where(kk + d * tk <= qq, bias, NEG_INF)

    zeros = jnp.zeros((HEAD_DIM, tq), BF16)
    ones = jnp.ones((L_ROWS, tk), BF16)
    s1 = jnp.sum(lq1_ref[...] * lk1_ref[...], axis=-1, keepdims=True)
    s2 = jnp.sum(lq2_ref[...] * lk2_ref[...], axis=-1, keepdims=True)
    lam = jnp.exp(s1) - jnp.exp(s2) + lam_init
    g = g_ref[pl.ds(h, 1), :]

    def n_tiles(i):
        return (i + 1) * n_diag

    def advance(i, j):
        last = j + 1 == n_tiles(i)
        return jnp.where(last, i + 1, i), jnp.where(last, 0, j + 1)

    def scores(i, j):
        i = jnp.minimum(i, n_q - 1)
        q = jnp.concatenate([qT_ref[0, i * q_chunks + c] for c in range(q_chunks)], axis=1)
        q_both = jnp.concatenate([jnp.concatenate([q[:HEAD_DIM], zeros], axis=0),
                                  jnp.concatenate([zeros, q[HEAD_DIM:]], axis=0)], axis=1)
        kb = k_ref[0, pl.ds(pl.multiple_of(j * tk, tk), tk), :]
        s = _dot(kb, q_both) + bm_sc[jnp.maximum(j - i * n_diag + 1, 0)]
        s_sc[...] = s
        mcur_sc[...] = jnp.max(s, axis=0, keepdims=True)

    def softmax(j):
        off = slope * (j * tk).astype(F32)
        m_old = jnp.where(j == 0, NEG_INF, m_sc[...])
        m_new = jnp.maximum(m_old, mcur_sc[...] + off)
        alpha_sc[...] = jnp.exp2(m_old - m_new)
        p_sc[...] = jnp.exp2(s_sc[...] - (m_new - off)).astype(BF16)
        m_sc[...] = m_new

    def accumulate(j):
        vT = [vT_ref[0, j * v_chunks + c] for c in range(v_chunks)]
        vT_aug = jnp.concatenate([jnp.concatenate(vT, axis=1), ones], axis=0)
        acc_sc[...] = alpha_sc[...] * acc_sc[...] + _dot(vT_aug, p_sc[...])

    def finalize(i):
        acc = acc_sc[...]
        oT = (acc[:V_DIM, :tq] / acc[V_DIM:V_DIM + 1, :tq]
              - lam * (acc[:V_DIM, tq:] / acc[V_DIM:V_DIM + 1, tq:]))
        o = oT.T
        y = o * lax.rsqrt(jnp.mean(o * o, axis=-1, keepdims=True) + LN_EPS) * g
        o_ref[0, pl.ds(pl.multiple_of(i * tq, tq), tq), :] = (y * (1.0 - lam_init)).astype(o_ref.dtype)

    m_sc[...] = jnp.full(m_sc.shape, NEG_INF, F32)
    p_sc[...] = jnp.zeros(p_sc.shape, BF16)
    alpha_sc[...] = jnp.ones(alpha_sc.shape, F32)
    acc_sc[...] = jnp.zeros(acc_sc.shape, F32)
    scores(0, 0)

    def body(t, carry):
        i_a, j_a, i_b, j_b, i_c, j_c = carry
        accumulate(j_c)
        softmax(j_b)
        scores(i_a, j_a)

        @pl.when(jnp.logical_and(t > 0, j_c + 1 == n_tiles(i_c)))
        def _():
            finalize(i_c)

        i_n, j_n = advance(i_a, j_a)
        return i_n, j_n, i_a, j_a, i_b, j_b

    zero = jnp.int32(0)
    i_1, j_1 = advance(zero, zero)
    n_trips = n_diag * n_q * (n_q + 1) // 2
    carry = lax.fori_loop(0, n_trips, body, (i_1, j_1, zero, zero, zero, zero))
    accumulate(carry[5])
    finalize(n_q - 1)


def _attention(lq1, lk1, lq2, lk2, subln_g, qT, k, vT, *, lam_init):
    B, S, _ = k.shape
    H = N_HEADS
    tq, tk = TQ_ATT, TK_ATT
    n_diag = tq // tk
    slopes = tuple(2.0 ** (-8.0 * i / H) for i in range(1, H + 1))
    lam_spec = pl.BlockSpec((1, HEAD_DIM), lambda h, b: (0, 0))
    return pl.pallas_call(
        functools.partial(_attn_kernel, slopes=slopes, lam_init=lam_init, n_q=S // tq),
        grid=(H, B),
        in_specs=[
            lam_spec, lam_spec, lam_spec, lam_spec,
            pl.BlockSpec((H, V_DIM), lambda h, b: (0, 0)),
            pl.BlockSpec((1, S // V_CHUNK, 2 * HEAD_DIM, V_CHUNK), lambda h, b: (b, 0, h, 0)),
            pl.BlockSpec((1, S, 2 * HEAD_DIM), lambda h, b: (b, 0, h)),
            pl.BlockSpec((1, S // V_CHUNK, V_DIM, V_CHUNK), lambda h, b: (b, 0, h, 0)),
        ],
        out_specs=pl.BlockSpec((1, S, V_DIM), lambda h, b: (b, 0, h)),
        out_shape=jax.ShapeDtypeStruct((B, S, H * V_DIM), BF16),
        scratch_shapes=[
            pltpu.VMEM((1 + n_diag, tk, 2 * tq), F32),
            pltpu.VMEM((tk, 2 * tq), F32),
            pltpu.VMEM((1, 2 * tq), F32),
            pltpu.VMEM((1, 2 * tq), F32),
            pltpu.VMEM((tk, 2 * tq), BF16),
            pltpu.VMEM((1, 2 * tq), F32),
            pltpu.VMEM((V_DIM + L_ROWS, 2 * tq), F32),
        ],
        compiler_params=pltpu.CompilerParams(
            dimension_semantics=("arbitrary", "arbitrary"),
            vmem_limit_bytes=VMEM_LIMIT_BYTES),
        name="attn",
    )(lq1, lk1, lq2, lk2, subln_g, qT, k, vT)


def _mix_kernel(h_ref, u_ref, halo_ref, o_ref, cw_ref, cg_ref, cb_ref, wco_ref, wao_ref,
                wgate_ref, wo_ref, g1_ref, b1_ref, out_ref, win_sc, cwb_sc, conv_sc, *, alpha):
    i = pl.program_id(1)
    tm = u_ref.shape[1]
    d = h_ref.shape[-1]

    @pl.when(jnp.logical_and(pl.program_id(0) == 0, i == 0))
    def _():
        for w in range(CONV_WIDTH):
            cwb_sc[w] = jnp.broadcast_to(cw_ref[w:w + 1, :], cwb_sc.shape[1:])

    halo = halo_ref[0]
    win_sc[0, 0:CONV_HALO, :] = jnp.where(i == 0, jnp.zeros_like(halo), halo)
    win_sc[0, CONV_HALO:CONV_HALO + tm, :] = u_ref[0]
    for s in range(1, SUBLANES):
        win_sc[s, 0:CONV_HALO + tm - SUBLANES, :] = win_sc[0, s:s + CONV_HALO + tm - SUBLANES, :]
    first = CONV_HALO - (CONV_WIDTH - 1)
    blocks = CONV_ROWS // SUBLANES
    for r in range(0, tm, CONV_ROWS):
        accs = [None] * blocks
        for w in range(CONV_WIDTH):
            a, s = divmod(first + w, SUBLANES)
            cw = cwb_sc[w]
            for rb in range(blocks):
                row = r + (a + rb) * SUBLANES
                term = win_sc[s, row:row + SUBLANES, :] * cw
                accs[rb] = term if accs[rb] is None else accs[rb] + term
        for rb in range(blocks):
            conv_sc[r + rb * SUBLANES:r + (rb + 1) * SUBLANES, :] = accs[rb]
    yc = _layer_norm(conv_sc[...], cg_ref[...], cb_ref[...])
    yc = yc * _sigmoid(yc)
    y_conv = _dot(yc.astype(BF16), wco_ref[...])
    y_attn = _dot(o_ref[0], wao_ref[...])

    h = h_ref[0]
    gates = _dot(h.astype(BF16), wgate_ref[...])
    merged = _sigmoid(gates[:, :d]) * y_conv + _sigmoid(gates[:, d:]) * y_attn
    r1 = alpha * h + _dot(merged.astype(BF16), wo_ref[...])
    out_ref[0] = _layer_norm(r1, g1_ref[...], b1_ref[...])


def _mix(h, u, o, conv_w, conv_g, conv_b, w_co, w_ao, w_gate, w_o, g1, b1, *, alpha):
    B, S, D = h.shape
    C = u.shape[-1]
    tm = TM_PROJ
    halo_blocks = tm // CONV_HALO
    const = lambda shape: pl.BlockSpec(shape, lambda bi, i: (0,) * len(shape))
    return pl.pallas_call(
        functools.partial(_mix_kernel, alpha=alpha),
        grid=(B, S // tm),
        in_specs=[
            pl.BlockSpec((1, tm, D), lambda bi, i: (bi, i, 0)),
            pl.BlockSpec((1, tm, C), lambda bi, i: (bi, i, 0)),
            pl.BlockSpec((1, CONV_HALO, C), lambda bi, i: (bi, jnp.maximum(i * halo_blocks - 1, 0), 0)),
            pl.BlockSpec((1, tm, o.shape[-1]), lambda bi, i: (bi, i, 0)),
            const(conv_w.shape), const((1, C)), const((1, C)),
            const(w_co.shape), const(w_ao.shape), const(w_gate.shape), const(w_o.shape),
            const((1, D)), const((1, D)),
        ],
        out_specs=pl.BlockSpec((1, tm, D), lambda bi, i: (bi, i, 0)),
        out_shape=jax.ShapeDtypeStruct((B, S, D), F32),
        scratch_shapes=[
            pltpu.VMEM((SUBLANES, CONV_HALO + tm, C), F32),
            pltpu.VMEM((CONV_WIDTH, SUBLANES, C), F32),
            pltpu.VMEM((tm, C), F32),
        ],
        compiler_params=pltpu.CompilerParams(
            dimension_semantics=("arbitrary", "arbitrary"), vmem_limit_bytes=VMEM_LIMIT_BYTES),
        name="mix",
    )(h, u, u, o, conv_w, conv_g, conv_b, w_co, w_ao, w_gate, w_o, g1, b1)


def _ffn_kernel(h_ref, p_ref, w1_ref, w2_ref, wpg_ref, wple_ref, g2_ref, b2_ref, out_ref,
                *, alpha, ff_chunk):
    h = h_ref[0]
    hb = h.astype(BF16)
    acc = alpha * h
    for c in range(0, w1_ref.shape[1], ff_chunk):
        a = jnp.maximum(_dot(hb, w1_ref[:, c:c + ff_chunk]), 0.0)
        acc = acc + _dot((a * a).astype(BF16), w2_ref[c:c + ff_chunk, :])
    ple = _dot(p_ref[0].astype(BF16), wple_ref[...])
    acc = acc + _sigmoid(_dot(hb, wpg_ref[...])) * ple
    out_ref[0] = _layer_norm(acc, g2_ref[...], b2_ref[...])


def _ffn(h, p, w1, w2, w_pg, w_ple, g2, b2, *, alpha):
    B, S, D = h.shape
    tm = TM_PROJ
    const = lambda shape: pl.BlockSpec(shape, lambda bi, i: (0,) * len(shape),
                                       pipeline_mode=pl.Buffered(1))
    return pl.pallas_call(
        functools.partial(_ffn_kernel, alpha=alpha, ff_chunk=D),
        grid=(B, S // tm),
        in_specs=[
            pl.BlockSpec((1, tm, D), lambda bi, i: (bi, i, 0)),
            pl.BlockSpec((1, tm, p.shape[-1]), lambda bi, i: (bi, i, 0)),
            const(w1.shape), const(w2.shape), const(w_pg.shape), const(w_ple.shape),
            const((1, D)), const((1, D)),
        ],
        out_specs=pl.BlockSpec((1, tm, D), lambda bi, i: (bi, i, 0)),
        out_shape=jax.ShapeDtypeStruct((B, S, D), F32),
        compiler_params=pltpu.CompilerParams(
            dimension_semantics=("arbitrary", "arbitrary"), vmem_limit_bytes=VMEM_LIMIT_BYTES),
        name="ffn",
    )(h, p, w1, w2, w_pg, w_ple, g2, b2)


def kernel(x, p, ln0_g, ln0_b, w_in, conv_w, conv_ln_g, conv_ln_b, w_conv_out, lambda_q1, lambda_k1, lambda_q2, lambda_k2, subln_g, w_attn_out, w_o, ln1_g, ln1_b, w_ff1, w_ff2, w_ple, w_ple_gate, ln2_g, ln2_b):
    depth = w_in.shape[0]
    c_conv = conv_w.shape[-1]
    n_glu = 2 * c_conv
    n_qk = N_HEADS * 2 * HEAD_DIM
    n_v = N_HEADS * V_DIM
    alpha = (2.0 * depth) ** 0.25
    row = lambda v: v.reshape(1, -1).astype(F32)

    h = x
    for i in range(depth):
        lam_init = 0.8 - 0.6 * math.exp(-0.3 * i)
        w = w_in[i].astype(BF16)
        w_glu = w[:, :n_glu]
        w_qT = w[:, n_glu:n_glu + n_qk].T
        w_k = w[:, n_glu + n_qk:n_glu + 2 * n_qk]
        w_vT = w[:, n_glu + 2 * n_qk:n_glu + 2 * n_qk + n_v].T
        w_gate = w[:, n_glu + 2 * n_qk + n_v:]

        h, u, qT, k, vT = _in_proj(h, row(ln0_g), row(ln0_b), w_glu, w_qT, w_k, w_vT,
                                   apply_ln=(i == 0))
        o = _attention(row(lambda_q1[i]), row(lambda_k1[i]), row(lambda_q2[i]), row(lambda_k2[i]),
                       subln_g[i].reshape(N_HEADS, V_DIM).astype(F32), qT, k, vT, lam_init=lam_init)
        h = _mix(h, u, o, conv_w[i].astype(F32), row(conv_ln_g[i]), row(conv_ln_b[i]),
                 w_conv_out[i].astype(BF16), w_attn_out[i].astype(BF16), w_gate,
                 w_o[i].astype(BF16), row(ln1_g[i]), row(ln1_b[i]), alpha=alpha)
        h = _ffn(h, p[i], w_ff1[i].astype(BF16), w_ff2[i].astype(BF16), w_ple_gate[i].astype(BF16),
                 w_ple[i].astype(BF16), row(ln2_g[i]), row(ln2_b[i]), alpha=alpha)
    return h
```

```python
import functools
import math

import jax
import jax.numpy as jnp
from jax import lax
from jax.experimental import pallas as pl
from jax.experimental.pallas import tpu as pltpu

F32 = jnp.float32
BF16 = jnp.bfloat16

LN_EPS = 1e-5
NEG_INF = -1e30
N_HEADS = 4
HEAD_DIM = 64
V_DIM = 2 * HEAD_DIM
CONV_WIDTH = 31
CONV_HALO = 32

V7X_VMEM_BYTES = 64 * 1024 * 1024
VMEM_LIMIT_BYTES = V7X_VMEM_BYTES * 3 // 4

TM_PROJ = 512
PROJ_ROWS = 256
TQ_ATT = 1024
TK_ATT = 512
V_CHUNK = TM_PROJ
CONV_ROWS = 32
MIX_ROWS = 256
SUBLANES = 8
LANES = 128
NT_DIMS = (((1,), (1,)), ((), ()))
LOG2E = math.log2(math.e)
L_ROWS = 16


def _layer_norm(x, g, b):
    mu = jnp.mean(x, axis=-1, keepdims=True)
    xc = x - mu
    var = jnp.mean(xc * xc, axis=-1, keepdims=True)
    return xc * lax.rsqrt(var + LN_EPS) * g + b


def _sigmoid(x):
    return 1.0 / (1.0 + jnp.exp(-x))


def _dot(a, b):
    return jnp.dot(a, b, preferred_element_type=F32)


def _in_proj_kernel(x_ref, g_ref, b_ref, wglu_ref, wqT_ref, wk_ref, wvT_ref,
                    h_ref, u_ref, qT_ref, k_ref, vT_ref, *, apply_ln, q_scale):
    c = u_ref.shape[-1]
    tm = x_ref.shape[1]
    for r in range(0, tm, PROJ_ROWS):
        rows = slice(r, r + PROJ_ROWS)
        x = x_ref[0, rows, :]
        h = _layer_norm(x, g_ref[...], b_ref[...]) if apply_ln else x
        h_ref[0, rows, :] = h
        hb = h.astype(BF16)
        zg = _dot(hb, wglu_ref[...])
        u_ref[0, rows, :] = zg[:, :c] * _sigmoid(zg[:, c:])
        k_ref[0, rows, :] = _dot(hb, wk_ref[...]).astype(BF16)
        qT = lax.dot_general(wqT_ref[...], hb, NT_DIMS, preferred_element_type=F32)
        qT_ref[0, 0, :, rows] = (qT * q_scale).astype(BF16)
        vT = lax.dot_general(wvT_ref[...], hb, NT_DIMS, preferred_element_type=F32)
        vT_ref[0, 0, :, rows] = vT.astype(BF16)


def _in_proj(x, g, b, w_glu, w_qT, w_k, w_vT, *, apply_ln):
    B, S, D = x.shape
    tm = TM_PROJ
    c2 = w_glu.shape[1]
    nq = w_qT.shape[0]
    nv = w_vT.shape[0]
    const = lambda shape: pl.BlockSpec(shape, lambda bi, i: (0,) * len(shape))
    return pl.pallas_call(
        functools.partial(_in_proj_kernel, apply_ln=apply_ln, q_scale=HEAD_DIM ** -0.5 * LOG2E),
        grid=(B, S // tm),
        in_specs=[
            pl.BlockSpec((1, tm, D), lambda bi, i: (bi, i, 0)),
            const((1, D)), const((1, D)),
            const(w_glu.shape), const(w_qT.shape), const(w_k.shape), const(w_vT.shape),
        ],
        out_specs=[
            pl.BlockSpec((1, tm, D), lambda bi, i: (bi, i, 0)),
            pl.BlockSpec((1, tm, c2 // 2), lambda bi, i: (bi, i, 0)),
            pl.BlockSpec((1, 1, nq, V_CHUNK), lambda bi, i: (bi, i, 0, 0)),
            pl.BlockSpec((1, tm, nq), lambda bi, i: (bi, i, 0)),
            pl.BlockSpec((1, 1, nv, V_CHUNK), lambda bi, i: (bi, i, 0, 0)),
        ],
        out_shape=[
            jax.ShapeDtypeStruct((B, S, D), F32),
            jax.ShapeDtypeStruct((B, S, c2 // 2), F32),
            jax.ShapeDtypeStruct((B, S // V_CHUNK, nq, V_CHUNK), BF16),
            jax.ShapeDtypeStruct((B, S, nq), BF16),
            jax.ShapeDtypeStruct((B, S // V_CHUNK, nv, V_CHUNK), BF16),
        ],
        compiler_params=pltpu.CompilerParams(
            dimension_semantics=("arbitrary", "arbitrary"), vmem_limit_bytes=VMEM_LIMIT_BYTES),
        name="in_proj",
    )(x, g, b, w_glu, w_qT, w_k, w_vT)


PLAIN, DIAG_FIRST, DIAG_SECOND = range(3)


def _attn_kernel(lq1_ref, lk1_ref, lq2_ref, lk2_ref, g_ref, qT_ref, k_ref, vT_ref, o_ref,
                 bcol_sc, ones_sc, tri_sc, s_sc, mcur_sc, m_sc, p_sc, alpha_sc, acc_sc,
                 *, slopes, lam_init, n_q):
    tq, tk = TQ_ATT, TK_ATT
    assert tq == 2 * tk
    q_chunks = tq // V_CHUNK
    v_chunks = tk // V_CHUNK
    h = pl.program_id(0)
    slope = jnp.float32(slopes[-1] * LOG2E)
    for hh in range(len(slopes) - 1):
        slope = jnp.where(h == hh, jnp.float32(slopes[hh] * LOG2E), slope)

    @pl.when(pl.program_id(1) == 0)
    def _():
        kpos = lax.broadcasted_iota(jnp.int32, (tk, LANES), 0).astype(F32)
        lane = lax.broadcasted_iota(jnp.int32, (tk, LANES), 1)
        bias = slope * kpos
        hi = bias.astype(BF16).astype(F32)
        mid = (bias - hi).astype(BF16).astype(F32)
        lo = bias - hi - mid
        cols = jnp.where(lane == 0, hi, jnp.where(lane == 1, mid, jnp.where(lane == 2, lo, 0.0)))
        bcol_sc[...] = cols.astype(BF16)
        row = lax.broadcasted_iota(jnp.int32, ones_sc.shape, 0)
        ones_sc[...] = jnp.where(row < 3, 1.0, 0.0).astype(BF16)
        kk = lax.broadcasted_iota(jnp.int32, (tk, tk), 0)
        qq = lax.broadcasted_iota(jnp.int32, (tk, tk), 1)
        tri_sc[...] = jnp.where(kk <= qq, 0.0, NEG_INF)

    zeros_q = jnp.zeros((HEAD_DIM, tk), BF16)
    ones_v = jnp.ones((L_ROWS, tk), BF16)
    s1 = jnp.sum(lq1_ref[...] * lk1_ref[...], axis=-1, keepdims=True)
    s2 = jnp.sum(lq2_ref[...] * lk2_ref[...], axis=-1, keepdims=True)
    lam = jnp.exp(s1) - jnp.exp(s2) + lam_init
    g = g_ref[pl.ds(h, 1), :]

    def scores(i, j, kind):
        q = [qT_ref[0, i * q_chunks + c] for c in range(q_chunks)]
        q = jnp.concatenate(q, axis=1)
        halves = [q[:, :tk], q[:, tk:]] if kind != DIAG_SECOND else [q[:, tk:]]
        top = jnp.concatenate([x[:HEAD_DIM] for x in halves] + [zeros_q] * len(halves), axis=1)
        bot = jnp.concatenate([zeros_q] * len(halves) + [x[HEAD_DIM:] for x in halves], axis=1)
        n = top.shape[1]
        rhs = jnp.concatenate([top, bot, ones_sc[:, :n]], axis=0)
        kb = k_ref[0, pl.ds(pl.multiple_of(j * tk, tk), tk), :]
        s = _dot(jnp.concatenate([kb, bcol_sc[...]], axis=1), rhs)
        blocks = [s[:, c * tk:(c + 1) * tk] for c in range(n // tk)]
        if kind == DIAG_FIRST:
            blocks = [blocks[0] + tri_sc[...], blocks[1], blocks[2] + tri_sc[...], blocks[3]]
        elif kind == DIAG_SECOND:
            empty = jnp.full((tk, tk), NEG_INF, F32)
            blocks = [empty, blocks[0] + tri_sc[...], empty, blocks[1] + tri_sc[...]]
        for c, blk in enumerate(blocks):
            s_sc[:, c * tk:(c + 1) * tk] = blk
            mcur_sc[:, c * tk:(c + 1) * tk] = jnp.max(blk, axis=0, keepdims=True)

    def softmax(j):
        off = slope * (j * tk).astype(F32)
        m_old = jnp.where(j == 0, NEG_INF, m_sc[...])
        m_new = jnp.maximum(m_old, mcur_sc[...] + off)
        alpha_sc[...] = jnp.exp2(m_old - m_new)
        p_sc[...] = jnp.exp2(s_sc[...] - (m_new - off)).astype(BF16)
        m_sc[...] = m_new

    def accumulate(j):
        vT = [vT_ref[0, j * v_chunks + c] for c in range(v_chunks)]
        vT_aug = jnp.concatenate([jnp.concatenate(vT, axis=1), ones_v], axis=0)
        acc_sc[...] = alpha_sc[...] * acc_sc[...] + _dot(vT_aug, p_sc[...])

    def finalize(i):
        acc = acc_sc[...]
        oT = (acc[:V_DIM, :tq] / acc[V_DIM:V_DIM + 1, :tq]
              - lam * (acc[:V_DIM, tq:] / acc[V_DIM:V_DIM + 1, tq:]))
        o = oT.T
        y = o * lax.rsqrt(jnp.mean(o * o, axis=-1, keepdims=True) + LN_EPS) * g
        o_ref[0, pl.ds(pl.multiple_of(i * tq, tq), tq), :] = (y * (1.0 - lam_init)).astype(o_ref.dtype)

    def trip(j_acc, j_soft, i_next, j_next, kind):
        j_acc, j_soft, i_next, j_next = (jnp.asarray(x, jnp.int32) for x in (j_acc, j_soft, i_next, j_next))
        accumulate(j_acc)
        softmax(j_soft)
        scores(i_next, j_next, kind)

    m_sc[...] = jnp.full(m_sc.shape, NEG_INF, F32)
    p_sc[...] = jnp.zeros(p_sc.shape, BF16)
    alpha_sc[...] = jnp.ones(alpha_sc.shape, F32)
    acc_sc[...] = jnp.zeros(acc_sc.shape, F32)
    scores(jnp.int32(0), jnp.int32(0), DIAG_FIRST)
    trip(0, 0, 0, 1, DIAG_SECOND)
    trip(0, 1, 1, 0, PLAIN)

    def tile(i, carry):
        trip(2 * i - 1, 0, i, 1, PLAIN)
        finalize(i - 1)

        def inner(m, c):
            j = 2 * m + 1
            trip(j - 1, j, i, j + 1, PLAIN)
            trip(j, j + 1, i, j + 2, PLAIN)
            return c

        lax.fori_loop(0, i - 1, inner, 0)
        trip(2 * i - 2, 2 * i - 1, i, 2 * i, DIAG_FIRST)
        trip(2 * i - 1, 2 * i, i, 2 * i + 1, DIAG_SECOND)
        trip(2 * i, 2 * i + 1, jnp.minimum(i + 1, n_q - 1), 0, PLAIN)
        return carry

    lax.fori_loop(1, n_q, tile, 0)
    accumulate(jnp.int32(2 * n_q - 1))
    finalize(jnp.int32(n_q - 1))


def _attention(lq1, lk1, lq2, lk2, subln_g, qT, k, vT, *, lam_init):
    B, S, _ = k.shape
    H = N_HEADS
    tq, tk = TQ_ATT, TK_ATT
    slopes = tuple(2.0 ** (-8.0 * i / H) for i in range(1, H + 1))
    lam_spec = pl.BlockSpec((1, HEAD_DIM), lambda h, b: (0, 0))
    return pl.pallas_call(
        functools.partial(_attn_kernel, slopes=slopes, lam_init=lam_init, n_q=S // tq),
        grid=(H, B),
        in_specs=[
            lam_spec, lam_spec, lam_spec, lam_spec,
            pl.BlockSpec((H, V_DIM), lambda h, b: (0, 0)),
            pl.BlockSpec((1, S // V_CHUNK, 2 * HEAD_DIM, V_CHUNK), lambda h, b: (b, 0, h, 0)),
            pl.BlockSpec((1, S, 2 * HEAD_DIM), lambda h, b: (b, 0, h)),
            pl.BlockSpec((1, S // V_CHUNK, V_DIM, V_CHUNK), lambda h, b: (b, 0, h, 0)),
        ],
        out_specs=pl.BlockSpec((1, S, V_DIM), lambda h, b: (b, 0, h)),
        out_shape=jax.ShapeDtypeStruct((B, S, H * V_DIM), BF16),
        scratch_shapes=[
            pltpu.VMEM((tk, LANES), BF16),
            pltpu.VMEM((LANES, 2 * tq), BF16),
            pltpu.VMEM((tk, tk), F32),
            pltpu.VMEM((tk, 2 * tq), F32),
            pltpu.VMEM((1, 2 * tq), F32),
            pltpu.VMEM((1, 2 * tq), F32),
            pltpu.VMEM((tk, 2 * tq), BF16),
            pltpu.VMEM((1, 2 * tq), F32),
            pltpu.VMEM((V_DIM + L_ROWS, 2 * tq), F32),
        ],
        compiler_params=pltpu.CompilerParams(
            dimension_semantics=("arbitrary", "arbitrary"),
            vmem_limit_bytes=VMEM_LIMIT_BYTES),
        name="attn",
    )(lq1, lk1, lq2, lk2, subln_g, qT, k, vT)


def _mix_kernel(h_ref, u_ref, halo_ref, o_ref, cw_ref, cg_ref, cb_ref, wco_ref, wao_ref,
                wgate_ref, wo_ref, g1_ref, b1_ref, out_ref, win_sc, cwb_sc, conv_sc, *, alpha):
    i = pl.program_id(1)
    tm = u_ref.shape[1]
    d = h_ref.shape[-1]

    @pl.when(jnp.logical_and(pl.program_id(0) == 0, i == 0))
    def _():
        for w in range(CONV_WIDTH):
            cwb_sc[w] = jnp.broadcast_to(cw_ref[w:w + 1, :], cwb_sc.shape[1:])

    halo = halo_ref[0]
    win_sc[0, 0:CONV_HALO, :] = jnp.where(i == 0, jnp.zeros_like(halo), halo)
    win_sc[0, CONV_HALO:CONV_HALO + tm, :] = u_ref[0]
    for s in range(1, SUBLANES):
        win_sc[s, 0:CONV_HALO + tm - SUBLANES, :] = win_sc[0, s:s + CONV_HALO + tm - SUBLANES, :]
    first = CONV_HALO - (CONV_WIDTH - 1)
    blocks = CONV_ROWS // SUBLANES

    def body(t, carry):
        r0 = pl.multiple_of(t * MIX_ROWS, MIX_ROWS)
        for r in range(0, MIX_ROWS, CONV_ROWS):
            accs = [None] * blocks
            for w in range(CONV_WIDTH):
                a, s = divmod(first + w, SUBLANES)
                cw = cwb_sc[w]
                for rb in range(blocks):
                    row = pl.multiple_of(r0 + r + (a + rb) * SUBLANES, SUBLANES)
                    term = win_sc[s, pl.ds(row, SUBLANES), :] * cw
                    accs[rb] = term if accs[rb] is None else accs[rb] + term
            for rb in range(blocks):
                conv_sc[r + rb * SUBLANES:r + (rb + 1) * SUBLANES, :] = accs[rb]
        rows = pl.ds(r0, MIX_ROWS)
        h = h_ref[0, rows, :]
        gates = _sigmoid(_dot(h.astype(BF16), wgate_ref[...]))
        y_attn = _dot(o_ref[0, rows, :], wao_ref[...])
        yc = _layer_norm(conv_sc[...], cg_ref[...], cb_ref[...])
        yc = yc * _sigmoid(yc)
        y_conv = _dot(yc.astype(BF16), wco_ref[...])
        merged = gates[:, :d] * y_conv + gates[:, d:] * y_attn
        r1 = alpha * h + _dot(merged.astype(BF16), wo_ref[...])
        out_ref[0, rows, :] = _layer_norm(r1, g1_ref[...], b1_ref[...])
        return carry

    lax.fori_loop(0, tm // MIX_ROWS, body, 0)


def _mix(h, u, o, conv_w, conv_g, conv_b, w_co, w_ao, w_gate, w_o, g1, b1, *, alpha):
    B, S, D = h.shape
    C = u.shape[-1]
    tm = TM_PROJ
    halo_blocks = tm // CONV_HALO
    const = lambda shape: pl.BlockSpec(shape, lambda bi, i: (0,) * len(shape))
    return pl.pallas_call(
        functools.partial(_mix_kernel, alpha=alpha),
        grid=(B, S // tm),
        in_specs=[
            pl.BlockSpec((1, tm, D), lambda bi, i: (bi, i, 0)),
            pl.BlockSpec((1, tm, C), lambda bi, i: (bi, i, 0)),
            pl.BlockSpec((1, CONV_HALO, C), lambda bi, i: (bi, jnp.maximum(i * halo_blocks - 1, 0), 0)),
            pl.BlockSpec((1, tm, o.shape[-1]), lambda bi, i: (bi, i, 0)),
            const(conv_w.shape), const((1, C)), const((1, C)),
            const(w_co.shape), const(w_ao.shape), const(w_gate.shape), const(w_o.shape),
            const((1, D)), const((1, D)),
        ],
        out_specs=pl.BlockSpec((1, tm, D), lambda bi, i: (bi, i, 0)),
        out_shape=jax.ShapeDtypeStruct((B, S, D), F32),
        scratch_shapes=[
            pltpu.VMEM((SUBLANES, CONV_HALO + tm, C), F32),
            pltpu.VMEM((CONV_WIDTH, SUBLANES, C), F32),
            pltpu.VMEM((MIX_ROWS, C), F32),
        ],
        compiler_params=pltpu.CompilerParams(
            dimension_semantics=("arbitrary", "arbitrary"), vmem_limit_bytes=VMEM_LIMIT_BYTES),
        name="mix",
    )(h, u, u, o, conv_w, conv_g, conv_b, w_co, w_ao, w_gate, w_o, g1, b1)


def _ffn_kernel(h_ref, p_ref, w1_ref, w2_ref, wpg_ref, wple_ref, g2_ref, b2_ref, out_ref,
                *, alpha, ff_chunk):
    h = h_ref[0]
    hb = h.astype(BF16)
    acc = alpha * h
    for c in range(0, w1_ref.shape[1], ff_chunk):
        a = jnp.maximum(_dot(hb, w1_ref[:, c:c + ff_chunk]), 0.0)
        acc = acc + _dot((a * a).astype(BF16), w2_ref[c:c + ff_chunk, :])
    ple = _dot(p_ref[0].astype(BF16), wple_ref[...])
    acc = acc + _sigmoid(_dot(hb, wpg_ref[...])) * ple
    out_ref[0] = _layer_norm(acc, g2_ref[...], b2_ref[...])


def _ffn(h, p, w1, w2, w_pg, w_ple, g2, b2, *, alpha):
    B, S, D = h.shape
    tm = TM_PROJ
    const = lambda shape: pl.BlockSpec(shape, lambda bi, i: (0,) * len(shape),
                                       pipeline_mode=pl.Buffered(1))
    return pl.pallas_call(
        functools.partial(_ffn_kernel, alpha=alpha, ff_chunk=D),
        grid=(B, S // tm),
        in_specs=[
            pl.BlockSpec((1, tm, D), lambda bi, i: (bi, i, 0)),
            pl.BlockSpec((1, tm, p.shape[-1]), lambda bi, i: (bi, i, 0)),
            const(w1.shape), const(w2.shape), const(w_pg.shape), const(w_ple.shape),
            const((1, D)), const((1, D)),
        ],
        out_specs=pl.BlockSpec((1, tm, D), lambda bi, i: (bi, i, 0)),
        out_shape=jax.ShapeDtypeStruct((B, S, D), F32),
        compiler_params=pltpu.CompilerParams(
            dimension_semantics=("arbitrary", "arbitrary"), vmem_limit_bytes=VMEM_LIMIT_BYTES),
        name="ffn",
    )(h, p, w1, w2, w_pg, w_ple, g2, b2)


def kernel(x, p, ln0_g, ln0_b, w_in, conv_w, conv_ln_g, conv_ln_b, w_conv_out, lambda_q1, lambda_k1, lambda_q2, lambda_k2, subln_g, w_attn_out, w_o, ln1_g, ln1_b, w_ff1, w_ff2, w_ple, w_ple_gate, ln2_g, ln2_b):
    depth = w_in.shape[0]
    c_conv = conv_w.shape[-1]
    n_glu = 2 * c_conv
    n_qk = N_HEADS * 2 * HEAD_DIM
    n_v = N_HEADS * V_DIM
    alpha = (2.0 * depth) ** 0.25
    row = lambda v: v.reshape(1, -1).astype(F32)

    h = x
    for i in range(depth):
        lam_init = 0.8 - 0.6 * math.exp(-0.3 * i)
        w = w_in[i].astype(BF16)
        w_glu = w[:, :n_glu]
        w_qT = w[:, n_glu:n_glu + n_qk].T
        w_k = w[:, n_glu + n_qk:n_glu + 2 * n_qk]
        w_vT = w[:, n_glu + 2 * n_qk:n_glu + 2 * n_qk + n_v].T
        w_gate = w[:, n_glu + 2 * n_qk + n_v:]

        h, u, qT, k, vT = _in_proj(h, row(ln0_g), row(ln0_b), w_glu, w_qT, w_k, w_vT,
                                   apply_ln=(i == 0))
        o = _attention(row(lambda_q1[i]), row(lambda_k1[i]), row(lambda_q2[i]), row(lambda_k2[i]),
                       subln_g[i].reshape(N_HEADS, V_DIM).astype(F32), qT, k, vT, lam_init=lam_init)
        h = _mix(h, u, o, conv_w[i].astype(F32), row(conv_ln_g[i]), row(conv_ln_b[i]),
                 w_conv_out[i].astype(BF16), w_attn_out[i].astype(BF16), w_gate,
                 w_o[i].astype(BF16), row(ln1_g[i]), row(ln1_b[i]), alpha=alpha)
        h = _ffn(h, p[i], w_ff1[i].astype(BF16), w_ff2[i].astype(BF16), w_ple_gate[i].astype(BF16),
                 w_ple[i].astype(BF16), row(ln2_g[i]), row(ln2_b[i]), alpha=alpha)
    return h
```

```python
import functools
import math

import jax
import jax.numpy as jnp
from jax import lax
from jax.experimental import pallas as pl
from jax.experimental.pallas import tpu as pltpu

F32 = jnp.float32
BF16 = jnp.bfloat16

LN_EPS = 1e-5
NEG_INF = -1e30
N_HEADS = 4
HEAD_DIM = 64
V_DIM = 2 * HEAD_DIM
CONV_WIDTH = 31
CONV_HALO = 32

V7X_VMEM_BYTES = 64 * 1024 * 1024
VMEM_LIMIT_BYTES = V7X_VMEM_BYTES * 3 // 4

TM_PROJ = 512
PROJ_ROWS = 256
TQ_ATT = 1024
TK_ATT = 512
V_CHUNK = TM_PROJ
CONV_ROWS = 32
MIX_ROWS = 256
SUBLANES = 8
LANES = 128
NT_DIMS = (((1,), (1,)), ((), ()))
LOG2E = math.log2(math.e)
L_ROWS = 16


def _layer_norm(x, g, b):
    mu = jnp.mean(x, axis=-1, keepdims=True)
    xc = x - mu
    var = jnp.mean(xc * xc, axis=-1, keepdims=True)
    return xc * lax.rsqrt(var + LN_EPS) * g + b


def _sigmoid(x):
    return 1.0 / (1.0 + jnp.exp(-x))


def _dot(a, b):
    return jnp.dot(a, b, preferred_element_type=F32)


def _in_proj_kernel(x_ref, g_ref, b_ref, wglu_ref, wqT_ref, wk_ref, wvT_ref,
                    h_ref, u_ref, qT_ref, k_ref, vT_ref, *, apply_ln, q_scale):
    c = u_ref.shape[-1]
    tm = x_ref.shape[1]
    for r in range(0, tm, PROJ_ROWS):
        rows = slice(r, r + PROJ_ROWS)
        x = x_ref[0, rows, :]
        h = _layer_norm(x, g_ref[...], b_ref[...]) if apply_ln else x
        h_ref[0, rows, :] = h
        hb = h.astype(BF16)
        zg = _dot(hb, wglu_ref[...])
        u_ref[0, rows, :] = zg[:, :c] * _sigmoid(zg[:, c:])
        k_ref[0, rows, :] = _dot(hb, wk_ref[...]).astype(BF16)
        qT = lax.dot_general(wqT_ref[...], hb, NT_DIMS, preferred_element_type=F32)
        qT_ref[0, 0, :, rows] = (qT * q_scale).astype(BF16)
        vT = lax.dot_general(wvT_ref[...], hb, NT_DIMS, preferred_element_type=F32)
        vT_ref[0, 0, :, rows] = vT.astype(BF16)


def _in_proj(x, g, b, w_glu, w_qT, w_k, w_vT, *, apply_ln):
    B, S, D = x.shape
    tm = TM_PROJ
    c2 = w_glu.shape[1]
    nq = w_qT.shape[0]
    nv = w_vT.shape[0]
    const = lambda shape: pl.BlockSpec(shape, lambda bi, i: (0,) * len(shape))
    return pl.pallas_call(
        functools.partial(_in_proj_kernel, apply_ln=apply_ln, q_scale=HEAD_DIM ** -0.5 * LOG2E),
        grid=(B, S // tm),
        in_specs=[
            pl.BlockSpec((1, tm, D), lambda bi, i: (bi, i, 0)),
            const((1, D)), const((1, D)),
            const(w_glu.shape), const(w_qT.shape), const(w_k.shape), const(w_vT.shape),
        ],
        out_specs=[
            pl.BlockSpec((1, tm, D), lambda bi, i: (bi, i, 0)),
            pl.BlockSpec((1, tm, c2 // 2), lambda bi, i: (bi, i, 0)),
            pl.BlockSpec((1, 1, nq, V_CHUNK), lambda bi, i: (bi, i, 0, 0)),
            pl.BlockSpec((1, tm, nq), lambda bi, i: (bi, i, 0)),
            pl.BlockSpec((1, 1, nv, V_CHUNK), lambda bi, i: (bi, i, 0, 0)),
        ],
        out_shape=[
            jax.ShapeDtypeStruct((B, S, D), F32),
            jax.ShapeDtypeStruct((B, S, c2 // 2), F32),
            jax.ShapeDtypeStruct((B, S // V_CHUNK, nq, V_CHUNK), BF16),
            jax.ShapeDtypeStruct((B, S, nq), BF16),
            jax.ShapeDtypeStruct((B, S // V_CHUNK, nv, V_CHUNK), BF16),
        ],
        compiler_params=pltpu.CompilerParams(
            dimension_semantics=("arbitrary", "arbitrary"), vmem_limit_bytes=VMEM_LIMIT_BYTES),
        name="in_proj",
    )(x, g, b, w_glu, w_qT, w_k, w_vT)


PLAIN, DIAG_FIRST, DIAG_SECOND = range(3)


def _attn_kernel(lq1_ref, lk1_ref, lq2_ref, lk2_ref, g_ref, qT_ref, k_ref, vT_ref, o_ref,
                 bcol_sc, ones_sc, tri_sc, s_sc, mcur_sc, m_sc, cnt_sc, step_sc, p_sc, alpha_sc, acc_sc,
                 *, slopes, lam_init, n_q):
    tq, tk = TQ_ATT, TK_ATT
    assert tq == 2 * tk
    q_chunks = tq // V_CHUNK
    v_chunks = tk // V_CHUNK
    h = pl.program_id(0)
    slope = jnp.float32(slopes[-1] * LOG2E)
    for hh in range(len(slopes) - 1):
        slope = jnp.where(h == hh, jnp.float32(slopes[hh] * LOG2E), slope)

    @pl.when(pl.program_id(1) == 0)
    def _():
        kpos = lax.broadcasted_iota(jnp.int32, (tk, LANES), 0).astype(F32)
        lane = lax.broadcasted_iota(jnp.int32, (tk, LANES), 1)
        bias = slope * kpos
        hi = bias.astype(BF16).astype(F32)
        mid = (bias - hi).astype(BF16).astype(F32)
        lo = bias - hi - mid
        cols = jnp.where(lane == 0, hi, jnp.where(lane == 1, mid, jnp.where(lane == 2, lo, 0.0)))
        bcol_sc[...] = cols.astype(BF16)
        row = lax.broadcasted_iota(jnp.int32, ones_sc.shape, 0)
        ones_sc[...] = jnp.where(row < 3, 1.0, 0.0).astype(BF16)
        kk = lax.broadcasted_iota(jnp.int32, (tk, tk), 0)
        qq = lax.broadcasted_iota(jnp.int32, (tk, tk), 1)
        tri_sc[...] = jnp.where(kk <= qq, 0.0, NEG_INF)

    step_sc[...] = jnp.full(step_sc.shape, slope * tk, F32)
    zeros_q = jnp.zeros((HEAD_DIM, tk), BF16)
    ones_v = jnp.ones((L_ROWS, tk), BF16)
    s1 = jnp.sum(lq1_ref[...] * lk1_ref[...], axis=-1, keepdims=True)
    s2 = jnp.sum(lq2_ref[...] * lk2_ref[...], axis=-1, keepdims=True)
    lam = jnp.exp(s1) - jnp.exp(s2) + lam_init
    g = g_ref[pl.ds(h, 1), :]

    def scores(i, j, kind):
        q = [qT_ref[0, i * q_chunks + c] for c in range(q_chunks)]
        q = jnp.concatenate(q, axis=1)
        halves = [q[:, :tk], q[:, tk:]] if kind != DIAG_SECOND else [q[:, tk:]]
        top = jnp.concatenate([x[:HEAD_DIM] for x in halves] + [zeros_q] * len(halves), axis=1)
        bot = jnp.concatenate([zeros_q] * len(halves) + [x[HEAD_DIM:] for x in halves], axis=1)
        n = top.shape[1]
        rhs = jnp.concatenate([top, bot, ones_sc[:, :n]], axis=0)
        kb = k_ref[0, pl.ds(pl.multiple_of(j * tk, tk), tk), :]
        s = _dot(jnp.concatenate([kb, bcol_sc[...]], axis=1), rhs)
        blocks = [s[:, c * tk:(c + 1) * tk] for c in range(n // tk)]
        if kind == DIAG_FIRST:
            blocks = [blocks[0] + tri_sc[...], blocks[1], blocks[2] + tri_sc[...], blocks[3]]
        elif kind == DIAG_SECOND:
            empty = jnp.full((tk, tk), NEG_INF, F32)
            blocks = [empty, blocks[0] + tri_sc[...], empty, blocks[1] + tri_sc[...]]
        for c, blk in enumerate(blocks):
            s_sc[:, c * tk:(c + 1) * tk] = blk
            mcur_sc[:, c * tk:(c + 1) * tk] = jnp.max(blk, axis=0, keepdims=True)

    def softmax(first):
        if first:
            cnt = jnp.zeros(cnt_sc.shape, F32)
            m_old = jnp.full(m_sc.shape, NEG_INF, F32)
        else:
            cnt = cnt_sc[...] + 1.0
            m_old = m_sc[...]
        cnt_sc[...] = cnt
        off = cnt * step_sc[...]
        m_new = jnp.maximum(m_old, mcur_sc[...] + off)
        alpha_sc[...] = jnp.exp2(m_old - m_new)
        p_sc[...] = jnp.exp2(s_sc[...] - (m_new - off)).astype(BF16)
        m_sc[...] = m_new

    def accumulate(j):
        vT = [vT_ref[0, j * v_chunks + c] for c in range(v_chunks)]
        vT_aug = jnp.concatenate([jnp.concatenate(vT, axis=1), ones_v], axis=0)
        acc_sc[...] = alpha_sc[...] * acc_sc[...] + _dot(vT_aug, p_sc[...])

    def finalize(i):
        acc = acc_sc[...]
        oT = (acc[:V_DIM, :tq] / acc[V_DIM:V_DIM + 1, :tq]
              - lam * (acc[:V_DIM, tq:] / acc[V_DIM:V_DIM + 1, tq:]))
        o = oT.T
        y = o * lax.rsqrt(jnp.mean(o * o, axis=-1, keepdims=True) + LN_EPS) * g
        o_ref[0, pl.ds(pl.multiple_of(i * tq, tq), tq), :] = (y * (1.0 - lam_init)).astype(o_ref.dtype)

    def trip(j_acc, first, i_next, j_next, kind):
        j_acc, i_next, j_next = (jnp.asarray(x, jnp.int32) for x in (j_acc, i_next, j_next))
        accumulate(j_acc)
        softmax(first)
        scores(i_next, j_next, kind)

    m_sc[...] = jnp.full(m_sc.shape, NEG_INF, F32)
    p_sc[...] = jnp.zeros(p_sc.shape, BF16)
    alpha_sc[...] = jnp.ones(alpha_sc.shape, F32)
    acc_sc[...] = jnp.zeros(acc_sc.shape, F32)
    scores(jnp.int32(0), jnp.int32(0), DIAG_FIRST)
    trip(0, True, 0, 1, DIAG_SECOND)
    trip(0, False, 1, 0, PLAIN)

    def tile(i, carry):
        trip(2 * i - 1, True, i, 1, PLAIN)
        finalize(i - 1)

        def inner(m, c):
            j = 2 * m + 1
            trip(j - 1, False, i, j + 1, PLAIN)
            trip(j, False, i, j + 2, PLAIN)
            return c

        lax.fori_loop(0, i - 1, inner, 0)
        trip(2 * i - 2, False, i, 2 * i, DIAG_FIRST)
        trip(2 * i - 1, False, i, 2 * i + 1, DIAG_SECOND)
        trip(2 * i, False, jnp.minimum(i + 1, n_q - 1), 0, PLAIN)
        return carry

    lax.fori_loop(1, n_q, tile, 0)
    accumulate(jnp.int32(2 * n_q - 1))
    finalize(jnp.int32(n_q - 1))


def _attention(lq1, lk1, lq2, lk2, subln_g, qT, k, vT, *, lam_init):
    B, S, _ = k.shape
    H = N_HEADS
    tq, tk = TQ_ATT, TK_ATT
    slopes = tuple(2.0 ** (-8.0 * i / H) for i in range(1, H + 1))
    lam_spec = pl.BlockSpec((1, HEAD_DIM), lambda h, b: (0, 0))
    return pl.pallas_call(
        functools.partial(_attn_kernel, slopes=slopes, lam_init=lam_init, n_q=S // tq),
        grid=(H, B),
        in_specs=[
            lam_spec, lam_spec, lam_spec, lam_spec,
            pl.BlockSpec((H, V_DIM), lambda h, b: (0, 0)),
            pl.BlockSpec((1, S // V_CHUNK, 2 * HEAD_DIM, V_CHUNK), lambda h, b: (b, 0, h, 0)),
            pl.BlockSpec((1, S, 2 * HEAD_DIM), lambda h, b: (b, 0, h)),
            pl.BlockSpec((1, S // V_CHUNK, V_DIM, V_CHUNK), lambda h, b: (b, 0, h, 0)),
        ],
        out_specs=pl.BlockSpec((1, S, V_DIM), lambda h, b: (b, 0, h)),
        out_shape=jax.ShapeDtypeStruct((B, S, H * V_DIM), BF16),
        scratch_shapes=[
            pltpu.VMEM((tk, LANES), BF16),
            pltpu.VMEM((LANES, 2 * tq), BF16),
            pltpu.VMEM((tk, tk), F32),
            pltpu.VMEM((tk, 2 * tq), F32),
            pltpu.VMEM((1, 2 * tq), F32),
            pltpu.VMEM((1, 2 * tq), F32),
            pltpu.VMEM((1, 2 * tq), F32),
            pltpu.VMEM((1, 2 * tq), F32),
            pltpu.VMEM((tk, 2 * tq), BF16),
            pltpu.VMEM((1, 2 * tq), F32),
            pltpu.VMEM((V_DIM + L_ROWS, 2 * tq), F32),
        ],
        compiler_params=pltpu.CompilerParams(
            dimension_semantics=("arbitrary", "arbitrary"),
            vmem_limit_bytes=VMEM_LIMIT_BYTES),
        name="attn",
    )(lq1, lk1, lq2, lk2, subln_g, qT, k, vT)


def _mix_kernel(h_ref, u_ref, halo_ref, o_ref, cw_ref, cg_ref, cb_ref, wco_ref, wao_ref,
                wgate_ref, wo_ref, g1_ref, b1_ref, out_ref, win_sc, cwb_sc, conv_sc, *, alpha):
    i = pl.program_id(1)
    tm = u_ref.shape[1]
    d = h_ref.shape[-1]

    @pl.when(jnp.logical_and(pl.program_id(0) == 0, i == 0))
    def _():
        for w in range(CONV_WIDTH):
            cwb_sc[w] = jnp.broadcast_to(cw_ref[w:w + 1, :], cwb_sc.shape[1:])

    halo = halo_ref[0]
    win_sc[0, 0:CONV_HALO, :] = jnp.where(i == 0, jnp.zeros_like(halo), halo)
    win_sc[0, CONV_HALO:CONV_HALO + tm, :] = u_ref[0]
    for s in range(1, SUBLANES):
        win_sc[s, 0:CONV_HALO + tm - SUBLANES, :] = win_sc[0, s:s + CONV_HALO + tm - SUBLANES, :]
    first = CONV_HALO - (CONV_WIDTH - 1)
    blocks = CONV_ROWS // SUBLANES

    def body(t, carry):
        r0 = pl.multiple_of(t * MIX_ROWS, MIX_ROWS)
        for r in range(0, MIX_ROWS, CONV_ROWS):
            accs = [None] * blocks
            for w in range(CONV_WIDTH):
                a, s = divmod(first + w, SUBLANES)
                cw = cwb_sc[w]
                for rb in range(blocks):
                    row = pl.multiple_of(r0 + r + (a + rb) * SUBLANES, SUBLANES)
                    term = win_sc[s, pl.ds(row, SUBLANES), :] * cw
                    accs[rb] = term if accs[rb] is None else accs[rb] + term
            for rb in range(blocks):
                conv_sc[r + rb * SUBLANES:r + (rb + 1) * SUBLANES, :] = accs[rb]
        rows = pl.ds(r0, MIX_ROWS)
        h = h_ref[0, rows, :]
        gates = _sigmoid(_dot(h.astype(BF16), wgate_ref[...]))
        y_attn = _dot(o_ref[0, rows, :], wao_ref[...])
        yc = _layer_norm(conv_sc[...], cg_ref[...], cb_ref[...])
        yc = yc * _sigmoid(yc)
        y_conv = _dot(yc.astype(BF16), wco_ref[...])
        merged = gates[:, :d] * y_conv + gates[:, d:] * y_attn
        r1 = alpha * h + _dot(merged.astype(BF16), wo_ref[...])
        out_ref[0, rows, :] = _layer_norm(r1, g1_ref[...], b1_ref[...])
        return carry

    lax.fori_loop(0, tm // MIX_ROWS, body, 0)


def _mix(h, u, o, conv_w, conv_g, conv_b, w_co, w_ao, w_gate, w_o, g1, b1, *, alpha):
    B, S, D = h.shape
    C = u.shape[-1]
    tm = TM_PROJ
    halo_blocks = tm // CONV_HALO
    const = lambda shape: pl.BlockSpec(shape, lambda bi, i: (0,) * len(shape))
    return pl.pallas_call(
        functools.partial(_mix_kernel, alpha=alpha),
        grid=(B, S // tm),
        in_specs=[
            pl.BlockSpec((1, tm, D), lambda bi, i: (bi, i, 0)),
            pl.BlockSpec((1, tm, C), lambda bi, i: (bi, i, 0)),
            pl.BlockSpec((1, CONV_HALO, C), lambda bi, i: (bi, jnp.maximum(i * halo_blocks - 1, 0), 0)),
            pl.BlockSpec((1, tm, o.shape[-1]), lambda bi, i: (bi, i, 0)),
            const(conv_w.shape), const((1, C)), const((1, C)),
            const(w_co.shape), const(w_ao.shape), const(w_gate.shape), const(w_o.shape),
            const((1, D)), const((1, D)),
        ],
        out_specs=pl.BlockSpec((1, tm, D), lambda bi, i: (bi, i, 0)),
        out_shape=jax.ShapeDtypeStruct((B, S, D), F32),
        scratch_shapes=[
            pltpu.VMEM((SUBLANES, CONV_HALO + tm, C), F32),
            pltpu.VMEM((CONV_WIDTH, SUBLANES, C), F32),
            pltpu.VMEM((MIX_ROWS, C), F32),
        ],
        compiler_params=pltpu.CompilerParams(
            dimension_semantics=("arbitrary", "arbitrary"), vmem_limit_bytes=VMEM_LIMIT_BYTES),
        name="mix",
    )(h, u, u, o, conv_w, conv_g, conv_b, w_co, w_ao, w_gate, w_o, g1, b1)


def _ffn_kernel(h_ref, p_ref, w1_ref, w2_ref, wpg_ref, wple_ref, g2_ref, b2_ref, out_ref,
                *, alpha, ff_chunk):
    for r in range(0, h_ref.shape[1], PROJ_ROWS):
        rows = slice(r, r + PROJ_ROWS)
        h = h_ref[0, rows, :]
        hb = h.astype(BF16)
        acc = alpha * h
        for c in range(0, w1_ref.shape[1], ff_chunk):
            a = jnp.maximum(_dot(hb, w1_ref[:, c:c + ff_chunk]), 0.0)
            acc = acc + _dot((a * a).astype(BF16), w2_ref[c:c + ff_chunk, :])
        ple = _dot(p_ref[0, rows, :].astype(BF16), wple_ref[...])
        acc = acc + _sigmoid(_dot(hb, wpg_ref[...])) * ple
        out_ref[0, rows, :] = _layer_norm(acc, g2_ref[...], b2_ref[...])


def _ffn(h, p, w1, w2, w_pg, w_ple, g2, b2, *, alpha):
    B, S, D = h.shape
    tm = TM_PROJ
    const = lambda shape: pl.BlockSpec(shape, lambda bi, i: (0,) * len(shape),
                                       pipeline_mode=pl.Buffered(1))
    return pl.pallas_call(
        functools.partial(_ffn_kernel, alpha=alpha, ff_chunk=D),
        grid=(B, S // tm),
        in_specs=[
            pl.BlockSpec((1, tm, D), lambda bi, i: (bi, i, 0)),
            pl.BlockSpec((1, tm, p.shape[-1]), lambda bi, i: (bi, i, 0)),
            const(w1.shape), const(w2.shape), const(w_pg.shape), const(w_ple.shape),
            const((1, D)), const((1, D)),
        ],
        out_specs=pl.BlockSpec((1, tm, D), lambda bi, i: (bi, i, 0)),
        out_shape=jax.ShapeDtypeStruct((B, S, D), F32),
        compiler_params=pltpu.CompilerParams(
            dimension_semantics=("arbitrary", "arbitrary"), vmem_limit_bytes=VMEM_LIMIT_BYTES),
        name="ffn",
    )(h, p, w1, w2, w_pg, w_ple, g2, b2)


def kernel(x, p, ln0_g, ln0_b, w_in, conv_w, conv_ln_g, conv_ln_b, w_conv_out, lambda_q1, lambda_k1, lambda_q2, lambda_k2, subln_g, w_attn_out, w_o, ln1_g, ln1_b, w_ff1, w_ff2, w_ple, w_ple_gate, ln2_g, ln2_b):
    depth = w_in.shape[0]
    c_conv = conv_w.shape[-1]
    n_glu = 2 * c_conv
    n_qk = N_HEADS * 2 * HEAD_DIM
    n_v = N_HEADS * V_DIM
    alpha = (2.0 * depth) ** 0.25
    row = lambda v: v.reshape(1, -1).astype(F32)

    h = x
    for i in range(depth):
        lam_init = 0.8 - 0.6 * math.exp(-0.3 * i)
        w = w_in[i].astype(BF16)
        w_glu = w[:, :n_glu]
        w_qT = w[:, n_glu:n_glu + n_qk].T
        w_k = w[:, n_glu + n_qk:n_glu + 2 * n_qk]
        w_vT = w[:, n_glu + 2 * n_qk:n_glu + 2 * n_qk + n_v].T
        w_gate = w[:, n_glu + 2 * n_qk + n_v:]

        h, u, qT, k, vT = _in_proj(h, row(ln0_g), row(ln0_b), w_glu, w_qT, w_k, w_vT,
                                   apply_ln=(i == 0))
        o = _attention(row(lambda_q1[i]), row(lambda_k1[i]), row(lambda_q2[i]), row(lambda_k2[i]),
                       subln_g[i].reshape(N_HEADS, V_DIM).astype(F32), qT, k, vT, lam_init=lam_init)
        h = _mix(h, u, o, conv_w[i].astype(F32), row(conv_ln_g[i]), row(conv_ln_b[i]),
                 w_conv_out[i].astype(BF16), w_attn_out[i].astype(BF16), w_gate,
                 w_o[i].astype(BF16), row(ln1_g[i]), row(ln1_b[i]), alpha=alpha)
        h = _ffn(h, p[i], w_ff1[i].astype(BF16), w_ff2[i].astype(BF16), w_ple_gate[i].astype(BF16),
                 w_ple[i].astype(BF16), row(ln2_g[i]), row(ln2_b[i]), alpha=alpha)
    return h
```

```python
import functools
import math

import jax
import jax.numpy as jnp
from jax import lax
from jax.experimental import pallas as pl
from jax.experimental.pallas import tpu as pltpu

F32 = jnp.float32
BF16 = jnp.bfloat16

LN_EPS = 1e-5
NEG_INF = -1e30
N_HEADS = 4
HEAD_DIM = 64
V_DIM = 2 * HEAD_DIM
CONV_WIDTH = 31
CONV_HALO = 32

V7X_VMEM_BYTES = 64 * 1024 * 1024
VMEM_LIMIT_BYTES = V7X_VMEM_BYTES * 3 // 4

TM_PROJ = 512
TM_WIDE = 1024
PROJ_ROWS = 256
T_ATT = 1024
V_CHUNK = TM_WIDE
CONV_ROWS = 32
MIX_ROWS = 256
SUBLANES = 8
LANES = 128
NT_DIMS = (((1,), (1,)), ((), ()))
LOG2E = math.log2(math.e)
L_ROWS = 16


def _layer_norm(x, g, b):
    mu = jnp.mean(x, axis=-1, keepdims=True)
    xc = x - mu
    var = jnp.mean(xc * xc, axis=-1, keepdims=True)
    return xc * lax.rsqrt(var + LN_EPS) * g + b


def _sigmoid(x):
    return 1.0 / (1.0 + jnp.exp(-x))


def _dot(a, b):
    return jnp.dot(a, b, preferred_element_type=F32)


def _in_proj_kernel(x_ref, g_ref, b_ref, wglu_ref, wqT_ref, wk_ref, wvT_ref,
                    h_ref, u_ref, qT_ref, k_ref, vT_ref, *, apply_ln, q_scale):
    c = u_ref.shape[-1]
    tm = x_ref.shape[1]
    for r in range(0, tm, PROJ_ROWS):
        rows = slice(r, r + PROJ_ROWS)
        x = x_ref[0, rows, :]
        h = _layer_norm(x, g_ref[...], b_ref[...]) if apply_ln else x
        h_ref[0, rows, :] = h
        hb = h.astype(BF16)
        zg = _dot(hb, wglu_ref[...])
        u_ref[0, rows, :] = zg[:, :c] * _sigmoid(zg[:, c:])
        k_ref[0, rows, :] = _dot(hb, wk_ref[...]).astype(BF16)
        qT = lax.dot_general(wqT_ref[...], hb, NT_DIMS, preferred_element_type=F32)
        qT_ref[0, 0, :, rows] = (qT * q_scale).astype(BF16)
        vT = lax.dot_general(wvT_ref[...], hb, NT_DIMS, preferred_element_type=F32)
        vT_ref[0, 0, :, rows] = vT.astype(BF16)


def _in_proj(x, g, b, w_glu, w_qT, w_k, w_vT, *, apply_ln):
    B, S, D = x.shape
    tm = TM_WIDE
    c2 = w_glu.shape[1]
    nq = w_qT.shape[0]
    nv = w_vT.shape[0]
    const = lambda shape: pl.BlockSpec(shape, lambda bi, i: (0,) * len(shape))
    return pl.pallas_call(
        functools.partial(_in_proj_kernel, apply_ln=apply_ln, q_scale=HEAD_DIM ** -0.5 * LOG2E),
        grid=(B, S // tm),
        in_specs=[
            pl.BlockSpec((1, tm, D), lambda bi, i: (bi, i, 0)),
            const((1, D)), const((1, D)),
            const(w_glu.shape), const(w_qT.shape), const(w_k.shape), const(w_vT.shape),
        ],
        out_specs=[
            pl.BlockSpec((1, tm, D), lambda bi, i: (bi, i, 0)),
            pl.BlockSpec((1, tm, c2 // 2), lambda bi, i: (bi, i, 0)),
            pl.BlockSpec((1, 1, nq, V_CHUNK), lambda bi, i: (bi, i, 0, 0)),
            pl.BlockSpec((1, tm, nq), lambda bi, i: (bi, i, 0)),
            pl.BlockSpec((1, 1, nv, V_CHUNK), lambda bi, i: (bi, i, 0, 0)),
        ],
        out_shape=[
            jax.ShapeDtypeStruct((B, S, D), F32),
            jax.ShapeDtypeStruct((B, S, c2 // 2), F32),
            jax.ShapeDtypeStruct((B, S // V_CHUNK, nq, V_CHUNK), BF16),
            jax.ShapeDtypeStruct((B, S, nq), BF16),
            jax.ShapeDtypeStruct((B, S // V_CHUNK, nv, V_CHUNK), BF16),
        ],
        compiler_params=pltpu.CompilerParams(
            dimension_semantics=("arbitrary", "arbitrary"), vmem_limit_bytes=VMEM_LIMIT_BYTES),
        name="in_proj",
    )(x, g, b, w_glu, w_qT, w_k, w_vT)


PLAIN, DIAG = range(2)


def _attn_kernel(lq1_ref, lk1_ref, lq2_ref, lk2_ref, g_ref, qT_ref, k_ref, vT_ref, o_ref,
                 bcol_sc, ones_sc, tri_sc, s_sc, mcur_sc, m_sc, cnt_sc, step_sc, p_sc, alpha_sc, acc_sc,
                 *, slopes, lam_init, n_q):
    t = T_ATT
    chunks = t // V_CHUNK
    h = pl.program_id(0)
    slope = jnp.float32(slopes[-1] * LOG2E)
    for hh in range(len(slopes) - 1):
        slope = jnp.where(h == hh, jnp.float32(slopes[hh] * LOG2E), slope)

    @pl.when(pl.program_id(1) == 0)
    def _():
        kpos = lax.broadcasted_iota(jnp.int32, (t, LANES), 0).astype(F32)
        lane = lax.broadcasted_iota(jnp.int32, (t, LANES), 1)
        bias = slope * kpos
        hi = bias.astype(BF16).astype(F32)
        mid = (bias - hi).astype(BF16).astype(F32)
        lo = bias - hi - mid
        cols = jnp.where(lane == 0, hi, jnp.where(lane == 1, mid, jnp.where(lane == 2, lo, 0.0)))
        bcol_sc[...] = cols.astype(BF16)
        row = lax.broadcasted_iota(jnp.int32, ones_sc.shape, 0)
        ones_sc[...] = jnp.where(row < 3, 1.0, 0.0).astype(BF16)
        kk = lax.broadcasted_iota(jnp.int32, (t, t), 0)
        qq = lax.broadcasted_iota(jnp.int32, (t, t), 1)
        tri_sc[...] = jnp.where(kk <= qq, 0.0, NEG_INF)

    step_sc[...] = jnp.full(step_sc.shape, slope * t, F32)
    zeros_q = jnp.zeros((HEAD_DIM, t), BF16)
    ones_v = jnp.ones((L_ROWS, t), BF16)
    s1 = jnp.sum(lq1_ref[...] * lk1_ref[...], axis=-1, keepdims=True)
    s2 = jnp.sum(lq2_ref[...] * lk2_ref[...], axis=-1, keepdims=True)
    lam = jnp.exp(s1) - jnp.exp(s2) + lam_init
    g = g_ref[pl.ds(h, 1), :]

    def scores(i, j, kind):
        q = jnp.concatenate([qT_ref[0, i * chunks + c] for c in range(chunks)], axis=1)
        top = jnp.concatenate([q[:HEAD_DIM], zeros_q], axis=1)
        bot = jnp.concatenate([zeros_q, q[HEAD_DIM:]], axis=1)
        rhs = jnp.concatenate([top, bot, ones_sc[...]], axis=0)
        kb = k_ref[0, pl.ds(pl.multiple_of(j * t, t), t), :]
        s = _dot(jnp.concatenate([kb, bcol_sc[...]], axis=1), rhs)
        for c in range(2):
            blk = s[:, c * t:(c + 1) * t]
            if kind == DIAG:
                blk = blk + tri_sc[...]
            s_sc[:, c * t:(c + 1) * t] = blk
            mcur_sc[:, c * t:(c + 1) * t] = jnp.max(blk, axis=0, keepdims=True)

    def softmax(first):
        if first:
            cnt = jnp.zeros(cnt_sc.shape, F32)
            m_old = jnp.full(m_sc.shape, NEG_INF, F32)
        else:
            cnt = cnt_sc[...] + 1.0
            m_old = m_sc[...]
        cnt_sc[...] = cnt
        off = cnt * step_sc[...]
        m_new = jnp.maximum(m_old, mcur_sc[...] + off)
        alpha_sc[...] = jnp.exp2(m_old - m_new)
        p_sc[...] = jnp.exp2(s_sc[...] - (m_new - off)).astype(BF16)
        m_sc[...] = m_new

    def accumulate(j):
        vT = [vT_ref[0, j * chunks + c] for c in range(chunks)]
        vT_aug = jnp.concatenate([jnp.concatenate(vT, axis=1), ones_v], axis=0)
        acc_sc[...] = alpha_sc[...] * acc_sc[...] + _dot(vT_aug, p_sc[...])

    def finalize(i):
        acc = acc_sc[...]
        oT = (acc[:V_DIM, :t] / acc[V_DIM:V_DIM + 1, :t]
              - lam * (acc[:V_DIM, t:] / acc[V_DIM:V_DIM + 1, t:]))
        o = oT.T
        y = o * lax.rsqrt(jnp.mean(o * o, axis=-1, keepdims=True) + LN_EPS) * g
        o_ref[0, pl.ds(pl.multiple_of(i * t, t), t), :] = (y * (1.0 - lam_init)).astype(o_ref.dtype)

    def trip(j_acc, first, i_next, j_next, kind):
        j_acc, i_next, j_next = (jnp.asarray(x, jnp.int32) for x in (j_acc, i_next, j_next))
        accumulate(j_acc)
        softmax(first)
        scores(i_next, j_next, kind)

    assert n_q >= 3
    m_sc[...] = jnp.full(m_sc.shape, NEG_INF, F32)
    p_sc[...] = jnp.zeros(p_sc.shape, BF16)
    alpha_sc[...] = jnp.ones(alpha_sc.shape, F32)
    acc_sc[...] = jnp.zeros(acc_sc.shape, F32)
    scores(jnp.int32(0), jnp.int32(0), DIAG)
    trip(0, True, 1, 0, PLAIN)
    trip(0, True, 1, 1, DIAG)
    finalize(jnp.int32(0))
    trip(0, False, 2, 0, PLAIN)

    def tile(i, carry):
        trip(i - 1, True, i, 1, PLAIN)
        finalize(i - 1)

        def inner(j, c):
            trip(j - 1, False, i, j + 1, PLAIN)
            return c

        lax.fori_loop(1, i - 1, inner, 0)
        trip(i - 2, False, i, i, DIAG)
        trip(i - 1, False, jnp.minimum(i + 1, n_q - 1), 0, PLAIN)
        return carry

    lax.fori_loop(2, n_q, tile, 0)
    accumulate(jnp.int32(n_q - 1))
    finalize(jnp.int32(n_q - 1))


def _attention(lq1, lk1, lq2, lk2, subln_g, qT, k, vT, *, lam_init):
    B, S, _ = k.shape
    H = N_HEADS
    t = T_ATT
    slopes = tuple(2.0 ** (-8.0 * i / H) for i in range(1, H + 1))
    lam_spec = pl.BlockSpec((1, HEAD_DIM), lambda h, b: (0, 0))
    return pl.pallas_call(
        functools.partial(_attn_kernel, slopes=slopes, lam_init=lam_init, n_q=S // t),
        grid=(H, B),
        in_specs=[
            lam_spec, lam_spec, lam_spec, lam_spec,
            pl.BlockSpec((H, V_DIM), lambda h, b: (0, 0)),
            pl.BlockSpec((1, S // V_CHUNK, 2 * HEAD_DIM, V_CHUNK), lambda h, b: (b, 0, h, 0)),
            pl.BlockSpec((1, S, 2 * HEAD_DIM), lambda h, b: (b, 0, h)),
            pl.BlockSpec((1, S // V_CHUNK, V_DIM, V_CHUNK), lambda h, b: (b, 0, h, 0)),
        ],
        out_specs=pl.BlockSpec((1, S, V_DIM), lambda h, b: (b, 0, h)),
        out_shape=jax.ShapeDtypeStruct((B, S, H * V_DIM), BF16),
        scratch_shapes=[
            pltpu.VMEM((t, LANES), BF16),
            pltpu.VMEM((LANES, 2 * t), BF16),
            pltpu.VMEM((t, t), F32),
            pltpu.VMEM((t, 2 * t), F32),
            pltpu.VMEM((1, 2 * t), F32),
            pltpu.VMEM((1, 2 * t), F32),
            pltpu.VMEM((1, 2 * t), F32),
            pltpu.VMEM((1, 2 * t), F32),
            pltpu.VMEM((t, 2 * t), BF16),
            pltpu.VMEM((1, 2 * t), F32),
            pltpu.VMEM((V_DIM + L_ROWS, 2 * t), F32),
        ],
        compiler_params=pltpu.CompilerParams(
            dimension_semantics=("arbitrary", "arbitrary"),
            vmem_limit_bytes=VMEM_LIMIT_BYTES),
        name="attn",
    )(lq1, lk1, lq2, lk2, subln_g, qT, k, vT)


def _mix_kernel(h_ref, u_ref, halo_ref, o_ref, cw_ref, cg_ref, cb_ref, wco_ref, wao_ref,
                wgate_ref, wo_ref, g1_ref, b1_ref, out_ref, win_sc, cwb_sc, conv_sc, *, alpha):
    i = pl.program_id(1)
    tm = u_ref.shape[1]
    d = h_ref.shape[-1]

    @pl.when(jnp.logical_and(pl.program_id(0) == 0, i == 0))
    def _():
        for w in range(CONV_WIDTH):
            cwb_sc[w] = jnp.broadcast_to(cw_ref[w:w + 1, :], cwb_sc.shape[1:])

    halo = halo_ref[0]
    win_sc[0, 0:CONV_HALO, :] = jnp.where(i == 0, jnp.zeros_like(halo), halo)
    win_sc[0, CONV_HALO:CONV_HALO + tm, :] = u_ref[0]
    for s in range(1, SUBLANES):
        win_sc[s, 0:CONV_HALO + tm - SUBLANES, :] = win_sc[0, s:s + CONV_HALO + tm - SUBLANES, :]
    first = CONV_HALO - (CONV_WIDTH - 1)
    blocks = CONV_ROWS // SUBLANES

    def body(t, carry):
        r0 = pl.multiple_of(t * MIX_ROWS, MIX_ROWS)
        for r in range(0, MIX_ROWS, CONV_ROWS):
            accs = [None] * blocks
            for w in range(CONV_WIDTH):
                a, s = divmod(first + w, SUBLANES)
                cw = cwb_sc[w]
                for rb in range(blocks):
                    row = pl.multiple_of(r0 + r + (a + rb) * SUBLANES, SUBLANES)
                    term = win_sc[s, pl.ds(row, SUBLANES), :] * cw
                    accs[rb] = term if accs[rb] is None else accs[rb] + term
            for rb in range(blocks):
                conv_sc[r + rb * SUBLANES:r + (rb + 1) * SUBLANES, :] = accs[rb]
        rows = pl.ds(r0, MIX_ROWS)
        h = h_ref[0, rows, :]
        gates = _sigmoid(_dot(h.astype(BF16), wgate_ref[...]))
        y_attn = _dot(o_ref[0, rows, :], wao_ref[...])
        yc = _layer_norm(conv_sc[...], cg_ref[...], cb_ref[...])
        yc = yc * _sigmoid(yc)
        y_conv = _dot(yc.astype(BF16), wco_ref[...])
        merged = gates[:, :d] * y_conv + gates[:, d:] * y_attn
        r1 = alpha * h + _dot(merged.astype(BF16), wo_ref[...])
        out_ref[0, rows, :] = _layer_norm(r1, g1_ref[...], b1_ref[...])
        return carry

    lax.fori_loop(0, tm // MIX_ROWS, body, 0)


def _mix(h, u, o, conv_w, conv_g, conv_b, w_co, w_ao, w_gate, w_o, g1, b1, *, alpha):
    B, S, D = h.shape
    C = u.shape[-1]
    tm = TM_PROJ
    halo_blocks = tm // CONV_HALO
    const = lambda shape: pl.BlockSpec(shape, lambda bi, i: (0,) * len(shape))
    return pl.pallas_call(
        functools.partial(_mix_kernel, alpha=alpha),
        grid=(B, S // tm),
        in_specs=[
            pl.BlockSpec((1, tm, D), lambda bi, i: (bi, i, 0)),
            pl.BlockSpec((1, tm, C), lambda bi, i: (bi, i, 0)),
            pl.BlockSpec((1, CONV_HALO, C), lambda bi, i: (bi, jnp.maximum(i * halo_blocks - 1, 0), 0)),
            pl.BlockSpec((1, tm, o.shape[-1]), lambda bi, i: (bi, i, 0)),
            const(conv_w.shape), const((1, C)), const((1, C)),
            const(w_co.shape), const(w_ao.shape), const(w_gate.shape), const(w_o.shape),
            const((1, D)), const((1, D)),
        ],
        out_specs=pl.BlockSpec((1, tm, D), lambda bi, i: (bi, i, 0)),
        out_shape=jax.ShapeDtypeStruct((B, S, D), F32),
        scratch_shapes=[
            pltpu.VMEM((SUBLANES, CONV_HALO + tm, C), F32),
            pltpu.VMEM((CONV_WIDTH, SUBLANES, C), F32),
            pltpu.VMEM((MIX_ROWS, C), F32),
        ],
        compiler_params=pltpu.CompilerParams(
            dimension_semantics=("arbitrary", "arbitrary"), vmem_limit_bytes=VMEM_LIMIT_BYTES),
        name="mix",
    )(h, u, u, o, conv_w, conv_g, conv_b, w_co, w_ao, w_gate, w_o, g1, b1)


def _ffn_kernel(h_ref, p_ref, w1_ref, w2_ref, wpg_ref, wple_ref, g2_ref, b2_ref, out_ref,
                *, alpha, ff_chunk):
    for r in range(0, h_ref.shape[1], PROJ_ROWS):
        rows = slice(r, r + PROJ_ROWS)
        h = h_ref[0, rows, :]
        hb = h.astype(BF16)
        acc = alpha * h
        for c in range(0, w1_ref.shape[1], ff_chunk):
            a = jnp.maximum(_dot(hb, w1_ref[:, c:c + ff_chunk]), 0.0)
            acc = acc + _dot((a * a).astype(BF16), w2_ref[c:c + ff_chunk, :])
        ple = _dot(p_ref[0, rows, :].astype(BF16), wple_ref[...])
        acc = acc + _sigmoid(_dot(hb, wpg_ref[...])) * ple
        out_ref[0, rows, :] = _layer_norm(acc, g2_ref[...], b2_ref[...])


def _ffn(h, p, w1, w2, w_pg, w_ple, g2, b2, *, alpha):
    B, S, D = h.shape
    tm = TM_WIDE
    const = lambda shape: pl.BlockSpec(shape, lambda bi, i: (0,) * len(shape),
                                       pipeline_mode=pl.Buffered(1))
    return pl.pallas_call(
        functools.partial(_ffn_kernel, alpha=alpha, ff_chunk=D),
        grid=(B, S // tm),
        in_specs=[
            pl.BlockSpec((1, tm, D), lambda bi, i: (bi, i, 0)),
            pl.BlockSpec((1, tm, p.shape[-1]), lambda bi, i: (bi, i, 0)),
            const(w1.shape), const(w2.shape), const(w_pg.shape), const(w_ple.shape),
            const((1, D)), const((1, D)),
        ],
        out_specs=pl.BlockSpec((1, tm, D), lambda bi, i: (bi, i, 0)),
        out_shape=jax.ShapeDtypeStruct((B, S, D), F32),
        compiler_params=pltpu.CompilerParams(
            dimension_semantics=("arbitrary", "arbitrary"), vmem_limit_bytes=VMEM_LIMIT_BYTES),
        name="ffn",
    )(h, p, w1, w2, w_pg, w_ple, g2, b2)


def kernel(x, p, ln0_g, ln0_b, w_in, conv_w, conv_ln_g, conv_ln_b, w_conv_out, lambda_q1, lambda_k1, lambda_q2, lambda_k2, subln_g, w_attn_out, w_o, ln1_g, ln1_b, w_ff1, w_ff2, w_ple, w_ple_gate, ln2_g, ln2_b):
    depth = w_in.shape[0]
    c_conv = conv_w.shape[-1]
    n_glu = 2 * c_conv
    n_qk = N_HEADS * 2 * HEAD_DIM
    n_v = N_HEADS * V_DIM
    alpha = (2.0 * depth) ** 0.25
    row = lambda v: v.reshape(1, -1).astype(F32)

    h = x
    for i in range(depth):
        lam_init = 0.8 - 0.6 * math.exp(-0.3 * i)
        w = w_in[i].astype(BF16)
        w_glu = w[:, :n_glu]
        w_qT = w[:, n_glu:n_glu + n_qk].T
        w_k = w[:, n_glu + n_qk:n_glu + 2 * n_qk]
        w_vT = w[:, n_glu + 2 * n_qk:n_glu + 2 * n_qk + n_v].T
        w_gate = w[:, n_glu + 2 * n_qk + n_v:]

        h, u, qT, k, vT = _in_proj(h, row(ln0_g), row(ln0_b), w_glu, w_qT, w_k, w_vT,
                                   apply_ln=(i == 0))
        o = _attention(row(lambda_q1[i]), row(lambda_k1[i]), row(lambda_q2[i]), row(lambda_k2[i]),
                       subln_g[i].reshape(N_HEADS, V_DIM).astype(F32), qT, k, vT, lam_init=lam_init)
        h = _mix(h, u, o, conv_w[i].astype(F32), row(conv_ln_g[i]), row(conv_ln_b[i]),
                 w_conv_out[i].astype(BF16), w_attn_out[i].astype(BF16), w_gate,
                 w_o[i].astype(BF16), row(ln1_g[i]), row(ln1_b[i]), alpha=alpha)
        h = _ffn(h, p[i], w_ff1[i].astype(BF16), w_ff2[i].astype(BF16), w_ple_gate[i].astype(BF16),
                 w_ple[i].astype(BF16), row(ln2_g[i]), row(ln2_b[i]), alpha=alpha)
    return h
```

```python
import functools
import math

import jax
import jax.numpy as jnp
from jax import lax
from jax.experimental import pallas as pl
from jax.experimental.pallas import tpu as pltpu

F32 = jnp.float32
BF16 = jnp.bfloat16

LN_EPS = 1e-5
NEG_INF = -1e30
N_HEADS = 4
HEAD_DIM = 64
V_DIM = 2 * HEAD_DIM
CONV_WIDTH = 31
CONV_HALO = 32

V7X_VMEM_BYTES = 64 * 1024 * 1024
VMEM_LIMIT_BYTES = V7X_VMEM_BYTES * 3 // 4

TM_PROJ = 512
TM_WIDE = 1024
PROJ_ROWS = 256
T_ATT = 1024
V_CHUNK = TM_WIDE
CONV_ROWS = 32
MIX_ROWS = 256
SUBLANES = 8
LANES = 128
NT_DIMS = (((1,), (1,)), ((), ()))
LOG2E = math.log2(math.e)
L_ROWS = 16


def _layer_norm(x, g, b):
    mu = jnp.mean(x, axis=-1, keepdims=True)
    xc = x - mu
    var = jnp.mean(xc * xc, axis=-1, keepdims=True)
    return xc * lax.rsqrt(var + LN_EPS) * g + b


def _sigmoid(x):
    return 1.0 / (1.0 + jnp.exp(-x))


def _dot(a, b):
    return jnp.dot(a, b, preferred_element_type=F32)


def _in_proj_kernel(x_ref, g_ref, b_ref, wglu_ref, wqT_ref, wk_ref, wvT_ref,
                    h_ref, u_ref, qT_ref, k_ref, vT_ref, *, apply_ln, q_scale):
    c = u_ref.shape[-1]
    tm = x_ref.shape[1]
    for r in range(0, tm, PROJ_ROWS):
        rows = slice(r, r + PROJ_ROWS)
        x = x_ref[0, rows, :]
        h = _layer_norm(x, g_ref[...], b_ref[...]) if apply_ln else x
        h_ref[0, rows, :] = h
        hb = h.astype(BF16)
        zg = _dot(hb, wglu_ref[...])
        u_ref[0, rows, :] = zg[:, :c] * _sigmoid(zg[:, c:])
        k_ref[0, rows, :] = _dot(hb, wk_ref[...]).astype(BF16)
        qT = lax.dot_general(wqT_ref[...], hb, NT_DIMS, preferred_element_type=F32)
        qT_ref[0, 0, :, rows] = (qT * q_scale).astype(BF16)
        vT = lax.dot_general(wvT_ref[...], hb, NT_DIMS, preferred_element_type=F32)
        vT_ref[0, 0, :, rows] = vT.astype(BF16)


def _in_proj(x, g, b, w_glu, w_qT, w_k, w_vT, *, apply_ln):
    B, S, D = x.shape
    tm = TM_WIDE
    c2 = w_glu.shape[1]
    nq = w_qT.shape[0]
    nv = w_vT.shape[0]
    const = lambda shape: pl.BlockSpec(shape, lambda bi, i: (0,) * len(shape))
    return pl.pallas_call(
        functools.partial(_in_proj_kernel, apply_ln=apply_ln, q_scale=HEAD_DIM ** -0.5 * LOG2E),
        grid=(B, S // tm),
        in_specs=[
            pl.BlockSpec((1, tm, D), lambda bi, i: (bi, i, 0)),
            const((1, D)), const((1, D)),
            const(w_glu.shape), const(w_qT.shape), const(w_k.shape), const(w_vT.shape),
        ],
        out_specs=[
            pl.BlockSpec((1, tm, D), lambda bi, i: (bi, i, 0)),
            pl.BlockSpec((1, tm, c2 // 2), lambda bi, i: (bi, i, 0)),
            pl.BlockSpec((1, 1, nq, V_CHUNK), lambda bi, i: (bi, i, 0, 0)),
            pl.BlockSpec((1, tm, nq), lambda bi, i: (bi, i, 0)),
            pl.BlockSpec((1, 1, nv, V_CHUNK), lambda bi, i: (bi, i, 0, 0)),
        ],
        out_shape=[
            jax.ShapeDtypeStruct((B, S, D), F32),
            jax.ShapeDtypeStruct((B, S, c2 // 2), F32),
            jax.ShapeDtypeStruct((B, S // V_CHUNK, nq, V_CHUNK), BF16),
            jax.ShapeDtypeStruct((B, S, nq), BF16),
            jax.ShapeDtypeStruct((B, S // V_CHUNK, nv, V_CHUNK), BF16),
        ],
        compiler_params=pltpu.CompilerParams(
            dimension_semantics=("arbitrary", "arbitrary"), vmem_limit_bytes=VMEM_LIMIT_BYTES),
        name="in_proj",
    )(x, g, b, w_glu, w_qT, w_k, w_vT)


PLAIN, DIAG = range(2)


def _attn_kernel(lq1_ref, lk1_ref, lq2_ref, lk2_ref, g_ref, qT_ref, k_ref, vT_ref, o_ref,
                 bcol_sc, ones_sc, tri_sc, s_sc, mcur_sc, m_sc, cnt_sc, step_sc, p_sc, alpha_sc, acc_sc,
                 *, slopes, lam_init, n_q):
    t = T_ATT
    hh = t // 2
    chunks = t // V_CHUNK
    h = pl.program_id(0)
    slope = jnp.float32(slopes[-1] * LOG2E)
    for head in range(len(slopes) - 1):
        slope = jnp.where(h == head, jnp.float32(slopes[head] * LOG2E), slope)

    @pl.when(pl.program_id(1) == 0)
    def _():
        kpos = lax.broadcasted_iota(jnp.int32, (t, LANES), 0).astype(F32)
        lane = lax.broadcasted_iota(jnp.int32, (t, LANES), 1)
        bias = slope * kpos
        hi = bias.astype(BF16).astype(F32)
        mid = (bias - hi).astype(BF16).astype(F32)
        lo = bias - hi - mid
        cols = jnp.where(lane == 0, hi, jnp.where(lane == 1, mid, jnp.where(lane == 2, lo, 0.0)))
        bcol_sc[...] = cols.astype(BF16)
        row = lax.broadcasted_iota(jnp.int32, ones_sc.shape, 0)
        ones_sc[...] = jnp.where(row < 3, 1.0, 0.0).astype(BF16)
        kk = lax.broadcasted_iota(jnp.int32, (hh, hh), 0)
        qq = lax.broadcasted_iota(jnp.int32, (hh, hh), 1)
        tri_sc[...] = jnp.where(kk <= qq, 0.0, NEG_INF)

    step_sc[...] = jnp.full(step_sc.shape, slope * t, F32)
    zeros_q = jnp.zeros((HEAD_DIM, t), BF16)
    ones_v = jnp.ones((L_ROWS, t), BF16)
    s1 = jnp.sum(lq1_ref[...] * lk1_ref[...], axis=-1, keepdims=True)
    s2 = jnp.sum(lq2_ref[...] * lk2_ref[...], axis=-1, keepdims=True)
    lam = jnp.exp(s1) - jnp.exp(s2) + lam_init
    g = g_ref[pl.ds(h, 1), :]

    def rhs_of(x):
        n = x.shape[1]
        top = jnp.concatenate([x[:HEAD_DIM], zeros_q[:, :n]], axis=1)
        bot = jnp.concatenate([zeros_q[:, :n], x[HEAD_DIM:]], axis=1)
        return jnp.concatenate([top, bot, ones_sc[:, :2 * n]], axis=0)

    def scores(i, j, kind):
        q = jnp.concatenate([qT_ref[0, i * chunks + c] for c in range(chunks)], axis=1)
        kb = k_ref[0, pl.ds(pl.multiple_of(j * t, t), t), :]
        lhs = jnp.concatenate([kb, bcol_sc[...]], axis=1)
        if kind == PLAIN:
            s = _dot(lhs, rhs_of(q))
            for c in range(2):
                blk = s[:, c * t:(c + 1) * t]
                s_sc[:, c * t:(c + 1) * t] = blk
                mcur_sc[:, c * t:(c + 1) * t] = jnp.max(blk, axis=0, keepdims=True)
            return
        s_hi = _dot(lhs, rhs_of(q[:, hh:]))
        s_lo = _dot(lhs[:hh], rhs_of(q[:, :hh]))
        tri = tri_sc[...]
        empty = jnp.full((hh, hh), NEG_INF, F32)
        for c in range(2):
            lo_lo = s_lo[:, c * hh:(c + 1) * hh] + tri
            lo_hi = s_hi[:hh, c * hh:(c + 1) * hh]
            hi_hi = s_hi[hh:, c * hh:(c + 1) * hh] + tri
            col = c * t
            s_sc[:hh, col:col + hh] = lo_lo
            s_sc[hh:, col:col + hh] = empty
            s_sc[:hh, col + hh:col + t] = lo_hi
            s_sc[hh:, col + hh:col + t] = hi_hi
            mcur_sc[:, col:col + hh] = jnp.max(lo_lo, axis=0, keepdims=True)
            mcur_sc[:, col + hh:col + t] = jnp.maximum(jnp.max(lo_hi, axis=0, keepdims=True),
                                                       jnp.max(hi_hi, axis=0, keepdims=True))

    def softmax(first):
        if first:
            cnt = jnp.zeros(cnt_sc.shape, F32)
            m_old = jnp.full(m_sc.shape, NEG_INF, F32)
        else:
            cnt = cnt_sc[...] + 1.0
            m_old = m_sc[...]
        cnt_sc[...] = cnt
        off = cnt * step_sc[...]
        m_new = jnp.maximum(m_old, mcur_sc[...] + off)
        alpha_sc[...] = jnp.exp2(m_old - m_new)
        p_sc[...] = jnp.exp2(s_sc[...] - (m_new - off)).astype(BF16)
        m_sc[...] = m_new

    def accumulate(j, kind):
        vT = [vT_ref[0, j * chunks + c] for c in range(chunks)]
        vT_aug = jnp.concatenate([jnp.concatenate(vT, axis=1), ones_v], axis=0)
        if kind == PLAIN:
            acc_sc[...] = alpha_sc[...] * acc_sc[...] + _dot(vT_aug, p_sc[...])
            return
        p_hi = jnp.concatenate([p_sc[:, hh:t], p_sc[:, t + hh:]], axis=1)
        p_lo = jnp.concatenate([p_sc[:hh, :hh], p_sc[:hh, t:t + hh]], axis=1)
        a_hi = _dot(vT_aug, p_hi)
        a_lo = _dot(vT_aug[:, :hh], p_lo)
        for c in range(2):
            lo = slice(c * t, c * t + hh)
            hi = slice(c * t + hh, (c + 1) * t)
            acc_sc[:, lo] = alpha_sc[:, lo] * acc_sc[:, lo] + a_lo[:, c * hh:(c + 1) * hh]
            acc_sc[:, hi] = alpha_sc[:, hi] * acc_sc[:, hi] + a_hi[:, c * hh:(c + 1) * hh]

    def finalize(i):
        acc = acc_sc[...]
        oT = (acc[:V_DIM, :t] / acc[V_DIM:V_DIM + 1, :t]
              - lam * (acc[:V_DIM, t:] / acc[V_DIM:V_DIM + 1, t:]))
        o = oT.T
        y = o * lax.rsqrt(jnp.mean(o * o, axis=-1, keepdims=True) + LN_EPS) * g
        o_ref[0, pl.ds(pl.multiple_of(i * t, t), t), :] = (y * (1.0 - lam_init)).astype(o_ref.dtype)

    def trip(j_acc, acc_kind, first, i_next, j_next, kind):
        j_acc, i_next, j_next = (jnp.asarray(x, jnp.int32) for x in (j_acc, i_next, j_next))
        accumulate(j_acc, acc_kind)
        softmax(first)
        scores(i_next, j_next, kind)

    assert n_q >= 3
    m_sc[...] = jnp.full(m_sc.shape, NEG_INF, F32)
    p_sc[...] = jnp.zeros(p_sc.shape, BF16)
    alpha_sc[...] = jnp.ones(alpha_sc.shape, F32)
    acc_sc[...] = jnp.zeros(acc_sc.shape, F32)
    scores(jnp.int32(0), jnp.int32(0), DIAG)
    trip(0, PLAIN, True, 1, 0, PLAIN)
    trip(0, DIAG, True, 1, 1, DIAG)
    finalize(jnp.int32(0))
    trip(0, PLAIN, False, 2, 0, PLAIN)

    def tile(i, carry):
        trip(i - 1, DIAG, True, i, 1, PLAIN)
        finalize(i - 1)

        def inner(j, c):
            trip(j - 1, PLAIN, False, i, j + 1, PLAIN)
            return c

        lax.fori_loop(1, i - 1, inner, 0)
        trip(i - 2, PLAIN, False, i, i, DIAG)
        trip(i - 1, PLAIN, False, jnp.minimum(i + 1, n_q - 1), 0, PLAIN)
        return carry

    lax.fori_loop(2, n_q, tile, 0)
    accumulate(jnp.int32(n_q - 1), DIAG)
    finalize(jnp.int32(n_q - 1))


def _attention(lq1, lk1, lq2, lk2, subln_g, qT, k, vT, *, lam_init):
    B, S, _ = k.shape
    H = N_HEADS
    t = T_ATT
    slopes = tuple(2.0 ** (-8.0 * i / H) for i in range(1, H + 1))
    lam_spec = pl.BlockSpec((1, HEAD_DIM), lambda h, b: (0, 0))
    return pl.pallas_call(
        functools.partial(_attn_kernel, slopes=slopes, lam_init=lam_init, n_q=S // t),
        grid=(H, B),
        in_specs=[
            lam_spec, lam_spec, lam_spec, lam_spec,
            pl.BlockSpec((H, V_DIM), lambda h, b: (0, 0)),
            pl.BlockSpec((1, S // V_CHUNK, 2 * HEAD_DIM, V_CHUNK), lambda h, b: (b, 0, h, 0)),
            pl.BlockSpec((1, S, 2 * HEAD_DIM), lambda h, b: (b, 0, h)),
            pl.BlockSpec((1, S // V_CHUNK, V_DIM, V_CHUNK), lambda h, b: (b, 0, h, 0)),
        ],
        out_specs=pl.BlockSpec((1, S, V_DIM), lambda h, b: (b, 0, h)),
        out_shape=jax.ShapeDtypeStruct((B, S, H * V_DIM), BF16),
        scratch_shapes=[
            pltpu.VMEM((t, LANES), BF16),
            pltpu.VMEM((LANES, 2 * t), BF16),
            pltpu.VMEM((t // 2, t // 2), F32),
            pltpu.VMEM((t, 2 * t), F32),
            pltpu.VMEM((1, 2 * t), F32),
            pltpu.VMEM((1, 2 * t), F32),
            pltpu.VMEM((1, 2 * t), F32),
            pltpu.VMEM((1, 2 * t), F32),
            pltpu.VMEM((t, 2 * t), BF16),
            pltpu.VMEM((1, 2 * t), F32),
            pltpu.VMEM((V_DIM + L_ROWS, 2 * t), F32),
        ],
        compiler_params=pltpu.CompilerParams(
            dimension_semantics=("arbitrary", "arbitrary"),
            vmem_limit_bytes=VMEM_LIMIT_BYTES),
        name="attn",
    )(lq1, lk1, lq2, lk2, subln_g, qT, k, vT)


def _mix_kernel(h_ref, u_ref, halo_ref, o_ref, cw_ref, cg_ref, cb_ref, wco_ref, wao_ref,
                wgate_ref, wo_ref, g1_ref, b1_ref, out_ref, win_sc, cwb_sc, conv_sc, *, alpha):
    i = pl.program_id(1)
    tm = u_ref.shape[1]
    d = h_ref.shape[-1]

    @pl.when(jnp.logical_and(pl.program_id(0) == 0, i == 0))
    def _():
        for w in range(CONV_WIDTH):
            cwb_sc[w] = jnp.broadcast_to(cw_ref[w:w + 1, :], cwb_sc.shape[1:])

    halo = halo_ref[0]
    win_sc[0, 0:CONV_HALO, :] = jnp.where(i == 0, jnp.zeros_like(halo), halo)
    win_sc[0, CONV_HALO:CONV_HALO + tm, :] = u_ref[0]
    for s in range(1, SUBLANES):
        win_sc[s, 0:CONV_HALO + tm - SUBLANES, :] = win_sc[0, s:s + CONV_HALO + tm - SUBLANES, :]
    first = CONV_HALO - (CONV_WIDTH - 1)
    blocks = CONV_ROWS // SUBLANES

    def body(t, carry):
        r0 = pl.multiple_of(t * MIX_ROWS, MIX_ROWS)
        for r in range(0, MIX_ROWS, CONV_ROWS):
            accs = [None] * blocks
            for w in range(CONV_WIDTH):
                a, s = divmod(first + w, SUBLANES)
                cw = cwb_sc[w]
                for rb in range(blocks):
                    row = pl.multiple_of(r0 + r + (a + rb) * SUBLANES, SUBLANES)
                    term = win_sc[s, pl.ds(row, SUBLANES), :] * cw
                    accs[rb] = term if accs[rb] is None else accs[rb] + term
            for rb in range(blocks):
                conv_sc[r + rb * SUBLANES:r + (rb + 1) * SUBLANES, :] = accs[rb]
        rows = pl.ds(r0, MIX_ROWS)
        h = h_ref[0, rows, :]
        gates = _sigmoid(_dot(h.astype(BF16), wgate_ref[...]))
        y_attn = _dot(o_ref[0, rows, :], wao_ref[...])
        yc = _layer_norm(conv_sc[...], cg_ref[...], cb_ref[...])
        yc = yc * _sigmoid(yc)
        y_conv = _dot(yc.astype(BF16), wco_ref[...])
        merged = gates[:, :d] * y_conv + gates[:, d:] * y_attn
        r1 = alpha * h + _dot(merged.astype(BF16), wo_ref[...])
        out_ref[0, rows, :] = _layer_norm(r1, g1_ref[...], b1_ref[...])
        return carry

    lax.fori_loop(0, tm // MIX_ROWS, body, 0)


def _mix(h, u, o, conv_w, conv_g, conv_b, w_co, w_ao, w_gate, w_o, g1, b1, *, alpha):
    B, S, D = h.shape
    C = u.shape[-1]
    tm = TM_PROJ
    halo_blocks = tm // CONV_HALO
    const = lambda shape: pl.BlockSpec(shape, lambda bi, i: (0,) * len(shape))
    return pl.pallas_call(
        functools.partial(_mix_kernel, alpha=alpha),
        grid=(B, S // tm),
        in_specs=[
            pl.BlockSpec((1, tm, D), lambda bi, i: (bi, i, 0)),
            pl.BlockSpec((1, tm, C), lambda bi, i: (bi, i, 0)),
            pl.BlockSpec((1, CONV_HALO, C), lambda bi, i: (bi, jnp.maximum(i * halo_blocks - 1, 0), 0)),
            pl.BlockSpec((1, tm, o.shape[-1]), lambda bi, i: (bi, i, 0)),
            const(conv_w.shape), const((1, C)), const((1, C)),
            const(w_co.shape), const(w_ao.shape), const(w_gate.shape), const(w_o.shape),
            const((1, D)), const((1, D)),
        ],
        out_specs=pl.BlockSpec((1, tm, D), lambda bi, i: (bi, i, 0)),
        out_shape=jax.ShapeDtypeStruct((B, S, D), F32),
        scratch_shapes=[
            pltpu.VMEM((SUBLANES, CONV_HALO + tm, C), F32),
            pltpu.VMEM((CONV_WIDTH, SUBLANES, C), F32),
            pltpu.VMEM((MIX_ROWS, C), F32),
        ],
        compiler_params=pltpu.CompilerParams(
            dimension_semantics=("arbitrary", "arbitrary"), vmem_limit_bytes=VMEM_LIMIT_BYTES),
        name="mix",
    )(h, u, u, o, conv_w, conv_g, conv_b, w_co, w_ao, w_gate, w_o, g1, b1)


def _ffn_kernel(h_ref, p_ref, w1_ref, w2_ref, wpg_ref, wple_ref, g2_ref, b2_ref, out_ref,
                *, alpha, ff_chunk):
    for r in range(0, h_ref.shape[1], PROJ_ROWS):
        rows = slice(r, r + PROJ_ROWS)
        h = h_ref[0, rows, :]
        hb = h.astype(BF16)
        acc = alpha * h
        for c in range(0, w1_ref.shape[1], ff_chunk):
            a = jnp.maximum(_dot(hb, w1_ref[:, c:c + ff_chunk]), 0.0)
            acc = acc + _dot((a * a).astype(BF16), w2_ref[c:c + ff_chunk, :])
        ple = _dot(p_ref[0, rows, :].astype(BF16), wple_ref[...])
        acc = acc + _sigmoid(_dot(hb, wpg_ref[...])) * ple
        out_ref[0, rows, :] = _layer_norm(acc, g2_ref[...], b2_ref[...])


def _ffn(h, p, w1, w2, w_pg, w_ple, g2, b2, *, alpha):
    B, S, D = h.shape
    tm = TM_WIDE
    const = lambda shape: pl.BlockSpec(shape, lambda bi, i: (0,) * len(shape),
                                       pipeline_mode=pl.Buffered(1))
    return pl.pallas_call(
        functools.partial(_ffn_kernel, alpha=alpha, ff_chunk=D),
        grid=(B, S // tm),
        in_specs=[
            pl.BlockSpec((1, tm, D), lambda bi, i: (bi, i, 0)),
            pl.BlockSpec((1, tm, p.shape[-1]), lambda bi, i: (bi, i, 0)),
            const(w1.shape), const(w2.shape), const(w_pg.shape), const(w_ple.shape),
            const((1, D)), const((1, D)),
        ],
        out_specs=pl.BlockSpec((1, tm, D), lambda bi, i: (bi, i, 0)),
        out_shape=jax.ShapeDtypeStruct((B, S, D), F32),
        compiler_params=pltpu.CompilerParams(
            dimension_semantics=("arbitrary", "arbitrary"), vmem_limit_bytes=VMEM_LIMIT_BYTES),
        name="ffn",
    )(h, p, w1, w2, w_pg, w_ple, g2, b2)


def kernel(x, p, ln0_g, ln0_b, w_in, conv_w, conv_ln_g, conv_ln_b, w_conv_out, lambda_q1, lambda_k1, lambda_q2, lambda_k2, subln_g, w_attn_out, w_o, ln1_g, ln1_b, w_ff1, w_ff2, w_ple, w_ple_gate, ln2_g, ln2_b):
    depth = w_in.shape[0]
    c_conv = conv_w.shape[-1]
    n_glu = 2 * c_conv
    n_qk = N_HEADS * 2 * HEAD_DIM
    n_v = N_HEADS * V_DIM
    alpha = (2.0 * depth) ** 0.25
    row = lambda v: v.reshape(1, -1).astype(F32)

    h = x
    for i in range(depth):
        lam_init = 0.8 - 0.6 * math.exp(-0.3 * i)
        w = w_in[i].astype(BF16)
        w_glu = w[:, :n_glu]
        w_qT = w[:, n_glu:n_glu + n_qk].T
        w_k = w[:, n_glu + n_qk:n_glu + 2 * n_qk]
        w_vT = w[:, n_glu + 2 * n_qk:n_glu + 2 * n_qk + n_v].T
        w_gate = w[:, n_glu + 2 * n_qk + n_v:]

        h, u, qT, k, vT = _in_proj(h, row(ln0_g), row(ln0_b), w_glu, w_qT, w_k, w_vT,
                                   apply_ln=(i == 0))
        o = _attention(row(lambda_q1[i]), row(lambda_k1[i]), row(lambda_q2[i]), row(lambda_k2[i]),
                       subln_g[i].reshape(N_HEADS, V_DIM).astype(F32), qT, k, vT, lam_init=lam_init)
        h = _mix(h, u, o, conv_w[i].astype(F32), row(conv_ln_g[i]), row(conv_ln_b[i]),
                 w_conv_out[i].astype(BF16), w_attn_out[i].astype(BF16), w_gate,
                 w_o[i].astype(BF16), row(ln1_g[i]), row(ln1_b[i]), alpha=alpha)
        h = _ffn(h, p[i], w_ff1[i].astype(BF16), w_ff2[i].astype(BF16), w_ple_gate[i].astype(BF16),
                 w_ple[i].astype(BF16), row(ln2_g[i]), row(ln2_b[i]), alpha=alpha)
    return h
```

```python
import functools
import math

import jax
import jax.numpy as jnp
from jax import lax
from jax.experimental import pallas as pl
from jax.experimental.pallas import tpu as pltpu

F32 = jnp.float32
BF16 = jnp.bfloat16

LN_EPS = 1e-5
NEG_INF = -1e30
N_HEADS = 4
HEAD_DIM = 64
V_DIM = 2 * HEAD_DIM
CONV_WIDTH = 31
CONV_HALO = 32

V7X_VMEM_BYTES = 64 * 1024 * 1024
VMEM_LIMIT_BYTES = V7X_VMEM_BYTES * 3 // 4

TM_PROJ = 512
TM_WIDE = 1024
PROJ_ROWS = 256
T_ATT = 1024
V_CHUNK = TM_WIDE
CONV_ROWS = 32
MIX_ROWS = 256
SUBLANES = 8
LANES = 128
NT_DIMS = (((1,), (1,)), ((), ()))
LOG2E = math.log2(math.e)
L_ROWS = 16


def _layer_norm(x, g, b):
    mu = jnp.mean(x, axis=-1, keepdims=True)
    xc = x - mu
    var = jnp.mean(xc * xc, axis=-1, keepdims=True)
    return xc * lax.rsqrt(var + LN_EPS) * g + b


def _sigmoid(x):
    return 1.0 / (1.0 + jnp.exp(-x))


def _dot(a, b):
    return jnp.dot(a, b, preferred_element_type=F32)


def _in_proj_kernel(x_ref, g_ref, b_ref, wglu_ref, wqT_ref, wk_ref, wvT_ref,
                    h_ref, u_ref, qT_ref, k_ref, vT_ref, *, apply_ln, q_scale):
    c = u_ref.shape[-1]
    tm = x_ref.shape[1]
    for r in range(0, tm, PROJ_ROWS):
        rows = slice(r, r + PROJ_ROWS)
        x = x_ref[0, rows, :]
        h = _layer_norm(x, g_ref[...], b_ref[...]) if apply_ln else x
        h_ref[0, rows, :] = h
        hb = h.astype(BF16)
        zg = _dot(hb, wglu_ref[...])
        u_ref[0, rows, :] = zg[:, :c] * _sigmoid(zg[:, c:])
        k_ref[0, rows, :] = _dot(hb, wk_ref[...]).astype(BF16)
        qT = lax.dot_general(wqT_ref[...], hb, NT_DIMS, preferred_element_type=F32)
        qT_ref[0, 0, :, rows] = (qT * q_scale).astype(BF16)
        vT = lax.dot_general(wvT_ref[...], hb, NT_DIMS, preferred_element_type=F32)
        vT_ref[0, 0, :, rows] = vT.astype(BF16)


def _in_proj(x, g, b, w_glu, w_qT, w_k, w_vT, *, apply_ln):
    B, S, D = x.shape
    tm = TM_WIDE
    c2 = w_glu.shape[1]
    nq = w_qT.shape[0]
    nv = w_vT.shape[0]
    const = lambda shape: pl.BlockSpec(shape, lambda bi, i: (0,) * len(shape))
    return pl.pallas_call(
        functools.partial(_in_proj_kernel, apply_ln=apply_ln, q_scale=HEAD_DIM ** -0.5 * LOG2E),
        grid=(B, S // tm),
        in_specs=[
            pl.BlockSpec((1, tm, D), lambda bi, i: (bi, i, 0)),
            const((1, D)), const((1, D)),
            const(w_glu.shape), const(w_qT.shape), const(w_k.shape), const(w_vT.shape),
        ],
        out_specs=[
            pl.BlockSpec((1, tm, D), lambda bi, i: (bi, i, 0)),
            pl.BlockSpec((1, tm, c2 // 2), lambda bi, i: (bi, i, 0)),
            pl.BlockSpec((1, 1, nq, V_CHUNK), lambda bi, i: (bi, i, 0, 0)),
            pl.BlockSpec((1, tm, nq), lambda bi, i: (bi, i, 0)),
            pl.BlockSpec((1, 1, nv, V_CHUNK), lambda bi, i: (bi, i, 0, 0)),
        ],
        out_shape=[
            jax.ShapeDtypeStruct((B, S, D), F32),
            jax.ShapeDtypeStruct((B, S, c2 // 2), F32),
            jax.ShapeDtypeStruct((B, S // V_CHUNK, nq, V_CHUNK), BF16),
            jax.ShapeDtypeStruct((B, S, nq), BF16),
            jax.ShapeDtypeStruct((B, S // V_CHUNK, nv, V_CHUNK), BF16),
        ],
        compiler_params=pltpu.CompilerParams(
            dimension_semantics=("arbitrary", "arbitrary"), vmem_limit_bytes=VMEM_LIMIT_BYTES),
        name="in_proj",
    )(x, g, b, w_glu, w_qT, w_k, w_vT)


PLAIN, DIAG = range(2)


def _attn_kernel(lq1_ref, lk1_ref, lq2_ref, lk2_ref, g_ref, qT_ref, k_ref, vT_ref, o_ref,
                 bcol_sc, ones_sc, tri_sc, s_sc, mcur_sc, m_sc, cnt_sc, step_sc, p_sc, alpha_sc, acc_sc,
                 *, slopes, lam_init, n_q):
    t = T_ATT
    hh = t // 2
    chunks = t // V_CHUNK
    h = pl.program_id(0)
    slope = jnp.float32(slopes[-1] * LOG2E)
    for head in range(len(slopes) - 1):
        slope = jnp.where(h == head, jnp.float32(slopes[head] * LOG2E), slope)

    @pl.when(pl.program_id(1) == 0)
    def _():
        kpos = lax.broadcasted_iota(jnp.int32, (t, LANES), 0).astype(F32)
        lane = lax.broadcasted_iota(jnp.int32, (t, LANES), 1)
        bias = slope * kpos
        hi = bias.astype(BF16).astype(F32)
        mid = (bias - hi).astype(BF16).astype(F32)
        lo = bias - hi - mid
        cols = jnp.where(lane == 0, hi, jnp.where(lane == 1, mid, jnp.where(lane == 2, lo, 0.0)))
        bcol_sc[...] = cols.astype(BF16)
        row = lax.broadcasted_iota(jnp.int32, ones_sc.shape, 0)
        ones_sc[...] = jnp.where(row < 3, 1.0, 0.0).astype(BF16)
        kk = lax.broadcasted_iota(jnp.int32, (hh, hh), 0)
        qq = lax.broadcasted_iota(jnp.int32, (hh, hh), 1)
        tri_sc[...] = jnp.where(kk <= qq, 0.0, NEG_INF)

    step_sc[...] = jnp.full(step_sc.shape, slope * t, F32)
    zeros_q = jnp.zeros((HEAD_DIM, t), BF16)
    ones_v = jnp.ones((L_ROWS, t), BF16)
    s1 = jnp.sum(lq1_ref[...] * lk1_ref[...], axis=-1, keepdims=True)
    s2 = jnp.sum(lq2_ref[...] * lk2_ref[...], axis=-1, keepdims=True)
    lam = jnp.exp(s1) - jnp.exp(s2) + lam_init
    g = g_ref[pl.ds(h, 1), :]

    def rhs_of(x):
        n = x.shape[1]
        top = jnp.concatenate([x[:HEAD_DIM], zeros_q[:, :n]], axis=1)
        bot = jnp.concatenate([zeros_q[:, :n], x[HEAD_DIM:]], axis=1)
        return jnp.concatenate([top, bot, ones_sc[:, :2 * n]], axis=0)

    def scores(i, j, kind):
        q = jnp.concatenate([qT_ref[0, i * chunks + c] for c in range(chunks)], axis=1)
        kb = k_ref[0, pl.ds(pl.multiple_of(j * t, t), t), :]
        lhs = jnp.concatenate([kb, bcol_sc[...]], axis=1)
        if kind == PLAIN:
            s = _dot(lhs, rhs_of(q))
            for c in range(2):
                blk = s[:, c * t:(c + 1) * t]
                s_sc[:, c * t:(c + 1) * t] = blk
                mcur_sc[:, c * t:(c + 1) * t] = jnp.max(blk, axis=0, keepdims=True)
            return
        s_hi = _dot(lhs, rhs_of(q[:, hh:]))
        s_lo = _dot(lhs[:hh], rhs_of(q[:, :hh]))
        tri = tri_sc[...]
        empty = jnp.full((hh, hh), NEG_INF, F32)
        for c in range(2):
            lo_lo = s_lo[:, c * hh:(c + 1) * hh] + tri
            lo_hi = s_hi[:hh, c * hh:(c + 1) * hh]
            hi_hi = s_hi[hh:, c * hh:(c + 1) * hh] + tri
            col = c * t
            s_sc[:hh, col:col + hh] = lo_lo
            s_sc[hh:, col:col + hh] = empty
            s_sc[:hh, col + hh:col + t] = lo_hi
            s_sc[hh:, col + hh:col + t] = hi_hi
            mcur_sc[:, col:col + hh] = jnp.max(lo_lo, axis=0, keepdims=True)
            mcur_sc[:, col + hh:col + t] = jnp.maximum(jnp.max(lo_hi, axis=0, keepdims=True),
                                                       jnp.max(hi_hi, axis=0, keepdims=True))

    def softmax(first):
        if first:
            cnt = jnp.zeros(cnt_sc.shape, F32)
            m_old = jnp.full(m_sc.shape, NEG_INF, F32)
        else:
            cnt = cnt_sc[...] + 1.0
            m_old = m_sc[...]
        cnt_sc[...] = cnt
        off = cnt * step_sc[...]
        m_new = jnp.maximum(m_old, mcur_sc[...] + off)
        alpha_sc[...] = jnp.exp2(m_old - m_new)
        p_sc[...] = jnp.exp2(s_sc[...] - (m_new - off)).astype(BF16)
        m_sc[...] = m_new

    def accumulate(j, kind):
        vT = [vT_ref[0, j * chunks + c] for c in range(chunks)]
        vT_aug = jnp.concatenate([jnp.concatenate(vT, axis=1), ones_v], axis=0)
        if kind == PLAIN:
            acc_sc[...] = alpha_sc[...] * acc_sc[...] + _dot(vT_aug, p_sc[...])
            return
        p_hi = jnp.concatenate([p_sc[:, hh:t], p_sc[:, t + hh:]], axis=1)
        p_lo = jnp.concatenate([p_sc[:hh, :hh], p_sc[:hh, t:t + hh]], axis=1)
        a_hi = _dot(vT_aug, p_hi)
        a_lo = _dot(vT_aug[:, :hh], p_lo)
        for c in range(2):
            lo = slice(c * t, c * t + hh)
            hi = slice(c * t + hh, (c + 1) * t)
            acc_sc[:, lo] = alpha_sc[:, lo] * acc_sc[:, lo] + a_lo[:, c * hh:(c + 1) * hh]
            acc_sc[:, hi] = alpha_sc[:, hi] * acc_sc[:, hi] + a_hi[:, c * hh:(c + 1) * hh]

    def finalize(i):
        acc = acc_sc[...]
        oT = (acc[:V_DIM, :t] / acc[V_DIM:V_DIM + 1, :t]
              - lam * (acc[:V_DIM, t:] / acc[V_DIM:V_DIM + 1, t:]))
        o = oT.T
        y = o * lax.rsqrt(jnp.mean(o * o, axis=-1, keepdims=True) + LN_EPS) * g
        o_ref[0, pl.ds(pl.multiple_of(i * t, t), t), :] = (y * (1.0 - lam_init)).astype(o_ref.dtype)

    def trip(j_acc, acc_kind, first, i_next, j_next, kind):
        j_acc, i_next, j_next = (jnp.asarray(x, jnp.int32) for x in (j_acc, i_next, j_next))
        accumulate(j_acc, acc_kind)
        softmax(first)
        scores(i_next, j_next, kind)

    assert n_q >= 3
    m_sc[...] = jnp.full(m_sc.shape, NEG_INF, F32)
    p_sc[...] = jnp.zeros(p_sc.shape, BF16)
    alpha_sc[...] = jnp.ones(alpha_sc.shape, F32)
    acc_sc[...] = jnp.zeros(acc_sc.shape, F32)
    scores(jnp.int32(0), jnp.int32(0), DIAG)
    trip(0, PLAIN, True, 1, 0, PLAIN)
    trip(0, DIAG, True, 1, 1, DIAG)
    finalize(jnp.int32(0))
    trip(0, PLAIN, False, 2, 0, PLAIN)

    def tile(i, carry):
        trip(i - 1, DIAG, True, i, 1, PLAIN)
        finalize(i - 1)

        def plain_trips(j, n):
            for x in range(n):
                trip(j + x - 1, PLAIN, False, i, j + x + 1, PLAIN)

        def inner(m, c):
            plain_trips(2 * m + 1, 2)
            return c

        doubles = (i - 2) // 2
        lax.fori_loop(0, doubles, inner, 0)

        @pl.when((i - 2) % 2 == 1)
        def _():
            plain_trips(2 * doubles + 1, 1)
        trip(i - 2, PLAIN, False, i, i, DIAG)
        trip(i - 1, PLAIN, False, jnp.minimum(i + 1, n_q - 1), 0, PLAIN)
        return carry

    lax.fori_loop(2, n_q, tile, 0)
    accumulate(jnp.int32(n_q - 1), DIAG)
    finalize(jnp.int32(n_q - 1))


def _attention(lq1, lk1, lq2, lk2, subln_g, qT, k, vT, *, lam_init):
    B, S, _ = k.shape
    H = N_HEADS
    t = T_ATT
    slopes = tuple(2.0 ** (-8.0 * i / H) for i in range(1, H + 1))
    lam_spec = pl.BlockSpec((1, HEAD_DIM), lambda h, b: (0, 0))
    return pl.pallas_call(
        functools.partial(_attn_kernel, slopes=slopes, lam_init=lam_init, n_q=S // t),
        grid=(H, B),
        in_specs=[
            lam_spec, lam_spec, lam_spec, lam_spec,
            pl.BlockSpec((H, V_DIM), lambda h, b: (0, 0)),
            pl.BlockSpec((1, S // V_CHUNK, 2 * HEAD_DIM, V_CHUNK), lambda h, b: (b, 0, h, 0)),
            pl.BlockSpec((1, S, 2 * HEAD_DIM), lambda h, b: (b, 0, h)),
            pl.BlockSpec((1, S // V_CHUNK, V_DIM, V_CHUNK), lambda h, b: (b, 0, h, 0)),
        ],
        out_specs=pl.BlockSpec((1, S, V_DIM), lambda h, b: (b, 0, h)),
        out_shape=jax.ShapeDtypeStruct((B, S, H * V_DIM), BF16),
        scratch_shapes=[
            pltpu.VMEM((t, LANES), BF16),
            pltpu.VMEM((LANES, 2 * t), BF16),
            pltpu.VMEM((t // 2, t // 2), F32),
            pltpu.VMEM((t, 2 * t), F32),
            pltpu.VMEM((1, 2 * t), F32),
            pltpu.VMEM((1, 2 * t), F32),
            pltpu.VMEM((1, 2 * t), F32),
            pltpu.VMEM((1, 2 * t), F32),
            pltpu.VMEM((t, 2 * t), BF16),
            pltpu.VMEM((1, 2 * t), F32),
            pltpu.VMEM((V_DIM + L_ROWS, 2 * t), F32),
        ],
        compiler_params=pltpu.CompilerParams(
            dimension_semantics=("arbitrary", "arbitrary"),
            vmem_limit_bytes=VMEM_LIMIT_BYTES),
        name="attn",
    )(lq1, lk1, lq2, lk2, subln_g, qT, k, vT)


def _mix_kernel(h_ref, u_ref, halo_ref, o_ref, cw_ref, cg_ref, cb_ref, wco_ref, wao_ref,
                wgate_ref, wo_ref, g1_ref, b1_ref, out_ref, win_sc, cwb_sc, conv_sc, *, alpha):
    i = pl.program_id(1)
    tm = u_ref.shape[1]
    d = h_ref.shape[-1]

    @pl.when(jnp.logical_and(pl.program_id(0) == 0, i == 0))
    def _():
        for w in range(CONV_WIDTH):
            cwb_sc[w] = jnp.broadcast_to(cw_ref[w:w + 1, :], cwb_sc.shape[1:])

    halo = halo_ref[0]
    win_sc[0, 0:CONV_HALO, :] = jnp.where(i == 0, jnp.zeros_like(halo), halo)
    win_sc[0, CONV_HALO:CONV_HALO + tm, :] = u_ref[0]
    for s in range(1, SUBLANES):
        win_sc[s, 0:CONV_HALO + tm - SUBLANES, :] = win_sc[0, s:s + CONV_HALO + tm - SUBLANES, :]
    first = CONV_HALO - (CONV_WIDTH - 1)
    blocks = CONV_ROWS // SUBLANES

    for r0 in range(0, tm, MIX_ROWS):
        for r in range(0, MIX_ROWS, CONV_ROWS):
            accs = [None] * blocks
            for w in range(CONV_WIDTH):
                a, s = divmod(first + w, SUBLANES)
                cw = cwb_sc[w]
                for rb in range(blocks):
                    row = r0 + r + (a + rb) * SUBLANES
                    term = win_sc[s, pl.ds(row, SUBLANES), :] * cw
                    accs[rb] = term if accs[rb] is None else accs[rb] + term
            for rb in range(blocks):
                conv_sc[r + rb * SUBLANES:r + (rb + 1) * SUBLANES, :] = accs[rb]
        rows = pl.ds(r0, MIX_ROWS)
        h = h_ref[0, rows, :]
        gates = _sigmoid(_dot(h.astype(BF16), wgate_ref[...]))
        y_attn = _dot(o_ref[0, rows, :], wao_ref[...])
        yc = _layer_norm(conv_sc[...], cg_ref[...], cb_ref[...])
        yc = yc * _sigmoid(yc)
        y_conv = _dot(yc.astype(BF16), wco_ref[...])
        merged = gates[:, :d] * y_conv + gates[:, d:] * y_attn
        r1 = alpha * h + _dot(merged.astype(BF16), wo_ref[...])
        out_ref[0, rows, :] = _layer_norm(r1, g1_ref[...], b1_ref[...])


def _mix(h, u, o, conv_w, conv_g, conv_b, w_co, w_ao, w_gate, w_o, g1, b1, *, alpha):
    B, S, D = h.shape
    C = u.shape[-1]
    tm = TM_PROJ
    halo_blocks = tm // CONV_HALO
    const = lambda shape: pl.BlockSpec(shape, lambda bi, i: (0,) * len(shape))
    return pl.pallas_call(
        functools.partial(_mix_kernel, alpha=alpha),
        grid=(B, S // tm),
        in_specs=[
            pl.BlockSpec((1, tm, D), lambda bi, i: (bi, i, 0)),
            pl.BlockSpec((1, tm, C), lambda bi, i: (bi, i, 0)),
            pl.BlockSpec((1, CONV_HALO, C), lambda bi, i: (bi, jnp.maximum(i * halo_blocks - 1, 0), 0)),
            pl.BlockSpec((1, tm, o.shape[-1]), lambda bi, i: (bi, i, 0)),
            const(conv_w.shape), const((1, C)), const((1, C)),
            const(w_co.shape), const(w_ao.shape), const(w_gate.shape), const(w_o.shape),
            const((1, D)), const((1, D)),
        ],
        out_specs=pl.BlockSpec((1, tm, D), lambda bi, i: (bi, i, 0)),
        out_shape=jax.ShapeDtypeStruct((B, S, D), F32),
        scratch_shapes=[
            pltpu.VMEM((SUBLANES, CONV_HALO + tm, C), F32),
            pltpu.VMEM((CONV_WIDTH, SUBLANES, C), F32),
            pltpu.VMEM((MIX_ROWS, C), F32),
        ],
        compiler_params=pltpu.CompilerParams(
            dimension_semantics=("arbitrary", "arbitrary"), vmem_limit_bytes=VMEM_LIMIT_BYTES),
        name="mix",
    )(h, u, u, o, conv_w, conv_g, conv_b, w_co, w_ao, w_gate, w_o, g1, b1)


def _ffn_kernel(h_ref, p_ref, w1_ref, w2_ref, wpg_ref, wple_ref, g2_ref, b2_ref, out_ref,
                *, alpha, ff_chunk):
    for r in range(0, h_ref.shape[1], PROJ_ROWS):
        rows = slice(r, r + PROJ_ROWS)
        h = h_ref[0, rows, :]
        hb = h.astype(BF16)
        acc = alpha * h
        for c in range(0, w1_ref.shape[1], ff_chunk):
            a = jnp.maximum(_dot(hb, w1_ref[:, c:c + ff_chunk]), 0.0)
            acc = acc + _dot((a * a).astype(BF16), w2_ref[c:c + ff_chunk, :])
        ple = _dot(p_ref[0, rows, :].astype(BF16), wple_ref[...])
        acc = acc + _sigmoid(_dot(hb, wpg_ref[...])) * ple
        out_ref[0, rows, :] = _layer_norm(acc, g2_ref[...], b2_ref[...])


def _ffn(h, p, w1, w2, w_pg, w_ple, g2, b2, *, alpha):
    B, S, D = h.shape
    tm = TM_WIDE
    const = lambda shape: pl.BlockSpec(shape, lambda bi, i: (0,) * len(shape),
                                       pipeline_mode=pl.Buffered(1))
    return pl.pallas_call(
        functools.partial(_ffn_kernel, alpha=alpha, ff_chunk=D),
        grid=(B, S // tm),
        in_specs=[
            pl.BlockSpec((1, tm, D), lambda bi, i: (bi, i, 0)),
            pl.BlockSpec((1, tm, p.shape[-1]), lambda bi, i: (bi, i, 0)),
            const(w1.shape), const(w2.shape), const(w_pg.shape), const(w_ple.shape),
            const((1, D)), const((1, D)),
        ],
        out_specs=pl.BlockSpec((1, tm, D), lambda bi, i: (bi, i, 0)),
        out_shape=jax.ShapeDtypeStruct((B, S, D), F32),
        compiler_params=pltpu.CompilerParams(
            dimension_semantics=("arbitrary", "arbitrary"), vmem_limit_bytes=VMEM_LIMIT_BYTES),
        name="ffn",
    )(h, p, w1, w2, w_pg, w_ple, g2, b2)


def kernel(x, p, ln0_g, ln0_b, w_in, conv_w, conv_ln_g, conv_ln_b, w_conv_out, lambda_q1, lambda_k1, lambda_q2, lambda_k2, subln_g, w_attn_out, w_o, ln1_g, ln1_b, w_ff1, w_ff2, w_ple, w_ple_gate, ln2_g, ln2_b):
    depth = w_in.shape[0]
    c_conv = conv_w.shape[-1]
    n_glu = 2 * c_conv
    n_qk = N_HEADS * 2 * HEAD_DIM
    n_v = N_HEADS * V_DIM
    alpha = (2.0 * depth) ** 0.25
    row = lambda v: v.reshape(1, -1).astype(F32)

    h = x
    for i in range(depth):
        lam_init = 0.8 - 0.6 * math.exp(-0.3 * i)
        w = w_in[i].astype(BF16)
        w_glu = w[:, :n_glu]
        w_qT = w[:, n_glu:n_glu + n_qk].T
        w_k = w[:, n_glu + n_qk:n_glu + 2 * n_qk]
        w_vT = w[:, n_glu + 2 * n_qk:n_glu + 2 * n_qk + n_v].T
        w_gate = w[:, n_glu + 2 * n_qk + n_v:]

        h, u, qT, k, vT = _in_proj(h, row(ln0_g), row(ln0_b), w_glu, w_qT, w_k, w_vT,
                                   apply_ln=(i == 0))
        o = _attention(row(lambda_q1[i]), row(lambda_k1[i]), row(lambda_q2[i]), row(lambda_k2[i]),
                       subln_g[i].reshape(N_HEADS, V_DIM).astype(F32), qT, k, vT, lam_init=lam_init)
        h = _mix(h, u, o, conv_w[i].astype(F32), row(conv_ln_g[i]), row(conv_ln_b[i]),
                 w_conv_out[i].astype(BF16), w_attn_out[i].astype(BF16), w_gate,
                 w_o[i].astype(BF16), row(ln1_g[i]), row(ln1_b[i]), alpha=alpha)
        h = _ffn(h, p[i], w_ff1[i].astype(BF16), w_ff2[i].astype(BF16), w_ple_gate[i].astype(BF16),
                 w_ple[i].astype(BF16), row(ln2_g[i]), row(ln2_b[i]), alpha=alpha)
    return h
```

```python
import functools
import math

import jax
import jax.numpy as jnp
from jax import lax
from jax.experimental import pallas as pl
from jax.experimental.pallas import tpu as pltpu

F32 = jnp.float32
BF16 = jnp.bfloat16

LN_EPS = 1e-5
NEG_INF = -1e30
N_HEADS = 4
HEAD_DIM = 64
V_DIM = 2 * HEAD_DIM
CONV_WIDTH = 31
CONV_HALO = 32

V7X_VMEM_BYTES = 64 * 1024 * 1024
VMEM_LIMIT_BYTES = V7X_VMEM_BYTES * 3 // 4

TM_PROJ = 512
TM_WIDE = 1024
PROJ_ROWS = 256
T_ATT = 1024
V_CHUNK = TM_WIDE
CONV_ROWS = 32
MIX_ROWS = 256
SUBLANES = 8
LANES = 128
NT_DIMS = (((1,), (1,)), ((), ()))
LOG2E = math.log2(math.e)
L_ROWS = 16


def _layer_norm(x, g, b):
    mu = jnp.mean(x, axis=-1, keepdims=True)
    xc = x - mu
    var = jnp.mean(xc * xc, axis=-1, keepdims=True)
    return xc * lax.rsqrt(var + LN_EPS) * g + b


def _sigmoid(x):
    return 1.0 / (1.0 + jnp.exp(-x))


def _dot(a, b):
    return jnp.dot(a, b, preferred_element_type=F32)


def _in_proj_kernel(x_ref, g_ref, b_ref, wglu_ref, wqT_ref, wk_ref, wvT_ref,
                    h_ref, u_ref, qT_ref, k_ref, vT_ref, *, apply_ln, q_scale):
    c = u_ref.shape[-1]
    tm = x_ref.shape[1]
    for r in range(0, tm, PROJ_ROWS):
        rows = slice(r, r + PROJ_ROWS)
        x = x_ref[0, rows, :]
        h = _layer_norm(x, g_ref[...], b_ref[...]) if apply_ln else x
        h_ref[0, rows, :] = h
        hb = h.astype(BF16)
        zg = _dot(hb, wglu_ref[...])
        u_ref[0, rows, :] = zg[:, :c] * _sigmoid(zg[:, c:])
        k_ref[0, rows, :] = _dot(hb, wk_ref[...]).astype(BF16)
        qT = lax.dot_general(wqT_ref[...], hb, NT_DIMS, preferred_element_type=F32)
        qT_ref[0, 0, :, rows] = (qT * q_scale).astype(BF16)
        vT = lax.dot_general(wvT_ref[...], hb, NT_DIMS, preferred_element_type=F32)
        vT_ref[0, 0, :, rows] = vT.astype(BF16)


def _in_proj(x, g, b, w_glu, w_qT, w_k, w_vT, *, apply_ln):
    B, S, D = x.shape
    tm = TM_WIDE
    c2 = w_glu.shape[1]
    nq = w_qT.shape[0]
    nv = w_vT.shape[0]
    const = lambda shape: pl.BlockSpec(shape, lambda bi, i: (0,) * len(shape))
    return pl.pallas_call(
        functools.partial(_in_proj_kernel, apply_ln=apply_ln, q_scale=HEAD_DIM ** -0.5 * LOG2E),
        grid=(B, S // tm),
        in_specs=[
            pl.BlockSpec((1, tm, D), lambda bi, i: (bi, i, 0)),
            const((1, D)), const((1, D)),
            const(w_glu.shape), const(w_qT.shape), const(w_k.shape), const(w_vT.shape),
        ],
        out_specs=[
            pl.BlockSpec((1, tm, D), lambda bi, i: (bi, i, 0)),
            pl.BlockSpec((1, tm, c2 // 2), lambda bi, i: (bi, i, 0)),
            pl.BlockSpec((1, 1, nq, V_CHUNK), lambda bi, i: (bi, i, 0, 0)),
            pl.BlockSpec((1, tm, nq), lambda bi, i: (bi, i, 0)),
            pl.BlockSpec((1, 1, nv, V_CHUNK), lambda bi, i: (bi, i, 0, 0)),
        ],
        out_shape=[
            jax.ShapeDtypeStruct((B, S, D), F32),
            jax.ShapeDtypeStruct((B, S, c2 // 2), F32),
            jax.ShapeDtypeStruct((B, S // V_CHUNK, nq, V_CHUNK), BF16),
            jax.ShapeDtypeStruct((B, S, nq), BF16),
            jax.ShapeDtypeStruct((B, S // V_CHUNK, nv, V_CHUNK), BF16),
        ],
        compiler_params=pltpu.CompilerParams(
            dimension_semantics=("arbitrary", "arbitrary"), vmem_limit_bytes=VMEM_LIMIT_BYTES),
        name="in_proj",
    )(x, g, b, w_glu, w_qT, w_k, w_vT)


PLAIN, DIAG = range(2)


def _attn_kernel(lq1_ref, lk1_ref, lq2_ref, lk2_ref, g_ref, qT_ref, k_ref, vT_ref, o_ref,
                 bcol_sc, ones_sc, tri_sc, s_sc, mcur_sc, m_sc, cnt_sc, step_sc, p_sc, alpha_sc, acc_sc,
                 *, slopes, lam_init, n_q):
    t = T_ATT
    hh = t // 2
    chunks = t // V_CHUNK
    h = pl.program_id(0)
    slope = jnp.float32(slopes[-1] * LOG2E)
    for head in range(len(slopes) - 1):
        slope = jnp.where(h == head, jnp.float32(slopes[head] * LOG2E), slope)

    @pl.when(pl.program_id(1) == 0)
    def _():
        kpos = lax.broadcasted_iota(jnp.int32, (t, LANES), 0).astype(F32)
        lane = lax.broadcasted_iota(jnp.int32, (t, LANES), 1)
        bias = slope * kpos
        hi = bias.astype(BF16).astype(F32)
        mid = (bias - hi).astype(BF16).astype(F32)
        lo = bias - hi - mid
        cols = jnp.where(lane == 0, hi, jnp.where(lane == 1, mid, jnp.where(lane == 2, lo, 0.0)))
        bcol_sc[...] = cols.astype(BF16)
        row = lax.broadcasted_iota(jnp.int32, ones_sc.shape, 0)
        ones_sc[...] = jnp.where(row < 3, 1.0, 0.0).astype(BF16)
        kk = lax.broadcasted_iota(jnp.int32, (hh, hh), 0)
        qq = lax.broadcasted_iota(jnp.int32, (hh, hh), 1)
        tri_sc[...] = jnp.where(kk <= qq, 0.0, NEG_INF)

    step_sc[...] = jnp.full(step_sc.shape, slope * t, F32)
    zeros_q = jnp.zeros((HEAD_DIM, t), BF16)
    ones_v = jnp.ones((L_ROWS, t), BF16)
    s1 = jnp.sum(lq1_ref[...] * lk1_ref[...], axis=-1, keepdims=True)
    s2 = jnp.sum(lq2_ref[...] * lk2_ref[...], axis=-1, keepdims=True)
    lam = jnp.exp(s1) - jnp.exp(s2) + lam_init
    g = g_ref[pl.ds(h, 1), :]

    def rhs_of(x):
        n = x.shape[1]
        top = jnp.concatenate([x[:HEAD_DIM], zeros_q[:, :n]], axis=1)
        bot = jnp.concatenate([zeros_q[:, :n], x[HEAD_DIM:]], axis=1)
        return jnp.concatenate([top, bot, ones_sc[:, :2 * n]], axis=0)

    def scores(i, j, kind):
        q = jnp.concatenate([qT_ref[0, i * chunks + c] for c in range(chunks)], axis=1)
        kb = k_ref[0, pl.ds(pl.multiple_of(j * t, t), t), :]
        lhs = jnp.concatenate([kb, bcol_sc[...]], axis=1)
        if kind == PLAIN:
            s = _dot(lhs, rhs_of(q))
            for c in range(2):
                blk = s[:, c * t:(c + 1) * t]
                s_sc[:, c * t:(c + 1) * t] = blk
                mcur_sc[:, c * t:(c + 1) * t] = jnp.max(blk, axis=0, keepdims=True)
            return
        s_hi = _dot(lhs, rhs_of(q[:, hh:]))
        s_lo = _dot(lhs[:hh], rhs_of(q[:, :hh]))
        tri = tri_sc[...]
        empty = jnp.full((hh, hh), NEG_INF, F32)
        for c in range(2):
            lo_lo = s_lo[:, c * hh:(c + 1) * hh] + tri
            lo_hi = s_hi[:hh, c * hh:(c + 1) * hh]
            hi_hi = s_hi[hh:, c * hh:(c + 1) * hh] + tri
            col = c * t
            s_sc[:hh, col:col + hh] = lo_lo
            s_sc[hh:, col:col + hh] = empty
            s_sc[:hh, col + hh:col + t] = lo_hi
            s_sc[hh:, col + hh:col + t] = hi_hi
            mcur_sc[:, col:col + hh] = jnp.max(lo_lo, axis=0, keepdims=True)
            mcur_sc[:, col + hh:col + t] = jnp.maximum(jnp.max(lo_hi, axis=0, keepdims=True),
                                                       jnp.max(hi_hi, axis=0, keepdims=True))

    def softmax(first):
        if first:
            cnt = jnp.zeros(cnt_sc.shape, F32)
            m_old = jnp.full(m_sc.shape, NEG_INF, F32)
        else:
            cnt = cnt_sc[...] + 1.0
            m_old = m_sc[...]
        cnt_sc[...] = cnt
        off = cnt * step_sc[...]
        m_new = jnp.maximum(m_old, mcur_sc[...] + off)
        alpha_sc[...] = jnp.exp2(m_old - m_new)
        p_sc[...] = jnp.exp2(s_sc[...] - (m_new - off)).astype(BF16)
        m_sc[...] = m_new

    def accumulate(j, kind):
        vT = [vT_ref[0, j * chunks + c] for c in range(chunks)]
        vT_aug = jnp.concatenate([jnp.concatenate(vT, axis=1), ones_v], axis=0)
        if kind == PLAIN:
            acc_sc[...] = alpha_sc[...] * acc_sc[...] + _dot(vT_aug, p_sc[...])
            return
        p_hi = jnp.concatenate([p_sc[:, hh:t], p_sc[:, t + hh:]], axis=1)
        p_lo = jnp.concatenate([p_sc[:hh, :hh], p_sc[:hh, t:t + hh]], axis=1)
        a_hi = _dot(vT_aug, p_hi)
        a_lo = _dot(vT_aug[:, :hh], p_lo)
        for c in range(2):
            lo = slice(c * t, c * t + hh)
            hi = slice(c * t + hh, (c + 1) * t)
            acc_sc[:, lo] = alpha_sc[:, lo] * acc_sc[:, lo] + a_lo[:, c * hh:(c + 1) * hh]
            acc_sc[:, hi] = alpha_sc[:, hi] * acc_sc[:, hi] + a_hi[:, c * hh:(c + 1) * hh]

    def finalize(i):
        acc = acc_sc[...]
        oT = (acc[:V_DIM, :t] / acc[V_DIM:V_DIM + 1, :t]
              - lam * (acc[:V_DIM, t:] / acc[V_DIM:V_DIM + 1, t:]))
        o = oT.T
        y = o * lax.rsqrt(jnp.mean(o * o, axis=-1, keepdims=True) + LN_EPS) * g
        o_ref[0, pl.ds(pl.multiple_of(i * t, t), t), :] = (y * (1.0 - lam_init)).astype(o_ref.dtype)

    def trip(j_acc, acc_kind, first, i_next, j_next, kind):
        j_acc, i_next, j_next = (jnp.asarray(x, jnp.int32) for x in (j_acc, i_next, j_next))
        accumulate(j_acc, acc_kind)
        softmax(first)
        scores(i_next, j_next, kind)

    assert n_q >= 3
    m_sc[...] = jnp.full(m_sc.shape, NEG_INF, F32)
    p_sc[...] = jnp.zeros(p_sc.shape, BF16)
    alpha_sc[...] = jnp.ones(alpha_sc.shape, F32)
    acc_sc[...] = jnp.zeros(acc_sc.shape, F32)
    scores(jnp.int32(0), jnp.int32(0), DIAG)
    trip(0, PLAIN, True, 1, 0, PLAIN)
    trip(0, DIAG, True, 1, 1, DIAG)
    finalize(jnp.int32(0))
    trip(0, PLAIN, False, 2, 0, PLAIN)

    def tile(i, carry):
        trip(i - 1, DIAG, True, i, 1, PLAIN)
        finalize(i - 1)

        def inner(j, c):
            trip(j - 1, PLAIN, False, i, j + 1, PLAIN)
            return c

        lax.fori_loop(1, i - 1, inner, 0)
        trip(i - 2, PLAIN, False, i, i, DIAG)
        trip(i - 1, PLAIN, False, jnp.minimum(i + 1, n_q - 1), 0, PLAIN)
        return carry

    lax.fori_loop(2, n_q, tile, 0)
    accumulate(jnp.int32(n_q - 1), DIAG)
    finalize(jnp.int32(n_q - 1))


def _attention(lq1, lk1, lq2, lk2, subln_g, qT, k, vT, *, lam_init):
    B, S, _ = k.shape
    H = N_HEADS
    t = T_ATT
    slopes = tuple(2.0 ** (-8.0 * i / H) for i in range(1, H + 1))
    lam_spec = pl.BlockSpec((1, HEAD_DIM), lambda h, b: (0, 0))
    return pl.pallas_call(
        functools.partial(_attn_kernel, slopes=slopes, lam_init=lam_init, n_q=S // t),
        grid=(H, B),
        in_specs=[
            lam_spec, lam_spec, lam_spec, lam_spec,
            pl.BlockSpec((H, V_DIM), lambda h, b: (0, 0)),
            pl.BlockSpec((1, S // V_CHUNK, 2 * HEAD_DIM, V_CHUNK), lambda h, b: (b, 0, h, 0)),
            pl.BlockSpec((1, S, 2 * HEAD_DIM), lambda h, b: (b, 0, h)),
            pl.BlockSpec((1, S // V_CHUNK, V_DIM, V_CHUNK), lambda h, b: (b, 0, h, 0)),
        ],
        out_specs=pl.BlockSpec((1, S, V_DIM), lambda h, b: (b, 0, h)),
        out_shape=jax.ShapeDtypeStruct((B, S, H * V_DIM), BF16),
        scratch_shapes=[
            pltpu.VMEM((t, LANES), BF16),
            pltpu.VMEM((LANES, 2 * t), BF16),
            pltpu.VMEM((t // 2, t // 2), F32),
            pltpu.VMEM((t, 2 * t), F32),
            pltpu.VMEM((1, 2 * t), F32),
            pltpu.VMEM((1, 2 * t), F32),
            pltpu.VMEM((1, 2 * t), F32),
            pltpu.VMEM((1, 2 * t), F32),
            pltpu.VMEM((t, 2 * t), BF16),
            pltpu.VMEM((1, 2 * t), F32),
            pltpu.VMEM((V_DIM + L_ROWS, 2 * t), F32),
        ],
        compiler_params=pltpu.CompilerParams(
            dimension_semantics=("arbitrary", "arbitrary"),
            vmem_limit_bytes=VMEM_LIMIT_BYTES),
        name="attn",
    )(lq1, lk1, lq2, lk2, subln_g, qT, k, vT)


def _mix_kernel(h_ref, u_ref, halo_ref, o_ref, cw_ref, cg_ref, cb_ref, wco_ref, wao_ref,
                wgate_ref, wo_ref, g1_ref, b1_ref, out_ref, win_sc, cwb_sc, conv_sc, *, alpha):
    i = pl.program_id(1)
    tm = u_ref.shape[1]
    d = h_ref.shape[-1]

    @pl.when(jnp.logical_and(pl.program_id(0) == 0, i == 0))
    def _():
        for w in range(CONV_WIDTH):
            cwb_sc[w] = jnp.broadcast_to(cw_ref[w:w + 1, :], cwb_sc.shape[1:])

    halo = halo_ref[0]
    win_sc[0, 0:CONV_HALO, :] = jnp.where(i == 0, jnp.zeros_like(halo), halo)
    win_sc[0, CONV_HALO:CONV_HALO + tm, :] = u_ref[0]
    for s in range(1, SUBLANES):
        win_sc[s, 0:CONV_HALO + tm - SUBLANES, :] = win_sc[0, s:s + CONV_HALO + tm - SUBLANES, :]
    first = CONV_HALO - (CONV_WIDTH - 1)
    blocks = CONV_ROWS // SUBLANES

    for r0 in range(0, tm, MIX_ROWS):
        for r in range(0, MIX_ROWS, CONV_ROWS):
            accs = [None] * blocks
            for w in range(CONV_WIDTH):
                a, s = divmod(first + w, SUBLANES)
                cw = cwb_sc[w]
                for rb in range(blocks):
                    row = r0 + r + (a + rb) * SUBLANES
                    term = win_sc[s, pl.ds(row, SUBLANES), :] * cw
                    accs[rb] = term if accs[rb] is None else accs[rb] + term
            for rb in range(blocks):
                conv_sc[r + rb * SUBLANES:r + (rb + 1) * SUBLANES, :] = accs[rb]
        rows = pl.ds(r0, MIX_ROWS)
        h = h_ref[0, rows, :]
        gates = _sigmoid(_dot(h.astype(BF16), wgate_ref[...]))
        y_attn = _dot(o_ref[0, rows, :], wao_ref[...])
        yc = _layer_norm(conv_sc[...], cg_ref[...], cb_ref[...])
        yc = yc * _sigmoid(yc)
        y_conv = _dot(yc.astype(BF16), wco_ref[...])
        merged = gates[:, :d] * y_conv + gates[:, d:] * y_attn
        r1 = alpha * h + _dot(merged.astype(BF16), wo_ref[...])
        out_ref[0, rows, :] = _layer_norm(r1, g1_ref[...], b1_ref[...])


def _mix(h, u, o, conv_w, conv_g, conv_b, w_co, w_ao, w_gate, w_o, g1, b1, *, alpha):
    B, S, D = h.shape
    C = u.shape[-1]
    tm = TM_PROJ
    halo_blocks = tm // CONV_HALO
    const = lambda shape: pl.BlockSpec(shape, lambda bi, i: (0,) * len(shape))
    return pl.pallas_call(
        functools.partial(_mix_kernel, alpha=alpha),
        grid=(B, S // tm),
        in_specs=[
            pl.BlockSpec((1, tm, D), lambda bi, i: (bi, i, 0)),
            pl.BlockSpec((1, tm, C), lambda bi, i: (bi, i, 0)),
            pl.BlockSpec((1, CONV_HALO, C), lambda bi, i: (bi, jnp.maximum(i * halo_blocks - 1, 0), 0)),
            pl.BlockSpec((1, tm, o.shape[-1]), lambda bi, i: (bi, i, 0)),
            const(conv_w.shape), const((1, C)), const((1, C)),
            const(w_co.shape), const(w_ao.shape), const(w_gate.shape), const(w_o.shape),
            const((1, D)), const((1, D)),
        ],
        out_specs=pl.BlockSpec((1, tm, D), lambda bi, i: (bi, i, 0)),
        out_shape=jax.ShapeDtypeStruct((B, S, D), F32),
        scratch_shapes=[
            pltpu.VMEM((SUBLANES, CONV_HALO + tm, C), F32),
            pltpu.VMEM((CONV_WIDTH, SUBLANES, C), F32),
            pltpu.VMEM((MIX_ROWS, C), F32),
        ],
        compiler_params=pltpu.CompilerParams(
            dimension_semantics=("arbitrary", "arbitrary"), vmem_limit_bytes=VMEM_LIMIT_BYTES),
        name="mix",
    )(h, u, u, o, conv_w, conv_g, conv_b, w_co, w_ao, w_gate, w_o, g1, b1)


def _ffn_kernel(h_ref, p_ref, w1_ref, w2_ref, wpg_ref, wple_ref, g2_ref, b2_ref, out_ref,
                *, alpha, ff_chunk):
    for r in range(0, h_ref.shape[1], PROJ_ROWS):
        rows = slice(r, r + PROJ_ROWS)
        h = h_ref[0, rows, :]
        hb = h.astype(BF16)
        acc = alpha * h
        for c in range(0, w1_ref.shape[1], ff_chunk):
            a = jnp.maximum(_dot(hb, w1_ref[:, c:c + ff_chunk]), 0.0)
            acc = acc + _dot((a * a).astype(BF16), w2_ref[c:c + ff_chunk, :])
        ple = _dot(p_ref[0, rows, :].astype(BF16), wple_ref[...])
        acc = acc + _sigmoid(_dot(hb, wpg_ref[...])) * ple
        out_ref[0, rows, :] = _layer_norm(acc, g2_ref[...], b2_ref[...])


def _ffn(h, p, w1, w2, w_pg, w_ple, g2, b2, *, alpha):
    B, S, D = h.shape
    tm = TM_WIDE
    const = lambda shape: pl.BlockSpec(shape, lambda bi, i: (0,) * len(shape),
                                       pipeline_mode=pl.Buffered(1))
    return pl.pallas_call(
        functools.partial(_ffn_kernel, alpha=alpha, ff_chunk=D),
        grid=(B, S // tm),
        in_specs=[
            pl.BlockSpec((1, tm, D), lambda bi, i: (bi, i, 0)),
            pl.BlockSpec((1, tm, p.shape[-1]), lambda bi, i: (bi, i, 0)),
            const(w1.shape), const(w2.shape), const(w_pg.shape), const(w_ple.shape),
            const((1, D)), const((1, D)),
        ],
        out_specs=pl.BlockSpec((1, tm, D), lambda bi, i: (bi, i, 0)),
        out_shape=jax.ShapeDtypeStruct((B, S, D), F32),
        compiler_params=pltpu.CompilerParams(
            dimension_semantics=("arbitrary", "arbitrary"), vmem_limit_bytes=VMEM_LIMIT_BYTES),
        name="ffn",
    )(h, p, w1, w2, w_pg, w_ple, g2, b2)


def kernel(x, p, ln0_g, ln0_b, w_in, conv_w, conv_ln_g, conv_ln_b, w_conv_out, lambda_q1, lambda_k1, lambda_q2, lambda_k2, subln_g, w_attn_out, w_o, ln1_g, ln1_b, w_ff1, w_ff2, w_ple, w_ple_gate, ln2_g, ln2_b):
    depth = w_in.shape[0]
    c_conv = conv_w.shape[-1]
    n_glu = 2 * c_conv
    n_qk = N_HEADS * 2 * HEAD_DIM
    n_v = N_HEADS * V_DIM
    alpha = (2.0 * depth) ** 0.25
    row = lambda v: v.reshape(1, -1).astype(F32)

    h = x
    for i in range(depth):
        lam_init = 0.8 - 0.6 * math.exp(-0.3 * i)
        w = w_in[i].astype(BF16)
        w_glu = w[:, :n_glu]
        w_qT = w[:, n_glu:n_glu + n_qk].T
        w_k = w[:, n_glu + n_qk:n_glu + 2 * n_qk]
        w_vT = w[:, n_glu + 2 * n_qk:n_glu + 2 * n_qk + n_v].T
        w_gate = w[:, n_glu + 2 * n_qk + n_v:]

        h, u, qT, k, vT = _in_proj(h, row(ln0_g), row(ln0_b), w_glu, w_qT, w_k, w_vT,
                                   apply_ln=(i == 0))
        o = _attention(row(lambda_q1[i]), row(lambda_k1[i]), row(lambda_q2[i]), row(lambda_k2[i]),
                       subln_g[i].reshape(N_HEADS, V_DIM).astype(F32), qT, k, vT, lam_init=lam_init)
        h = _mix(h, u, o, conv_w[i].astype(F32), row(conv_ln_g[i]), row(conv_ln_b[i]),
                 w_conv_out[i].astype(BF16), w_attn_out[i].astype(BF16), w_gate,
                 w_o[i].astype(BF16), row(ln1_g[i]), row(ln1_b[i]), alpha=alpha)
        h = _ffn(h, p[i], w_ff1[i].astype(BF16), w_ff2[i].astype(BF16), w_ple_gate[i].astype(BF16),
                 w_ple[i].astype(BF16), row(ln2_g[i]), row(ln2_b[i]), alpha=alpha)
    return h
```

```python
import functools
import math

import jax
import jax.numpy as jnp
from jax import lax
from jax.experimental import pallas as pl
from jax.experimental.pallas import tpu as pltpu

F32 = jnp.float32
BF16 = jnp.bfloat16

LN_EPS = 1e-5
NEG_INF = -1e30
N_HEADS = 4
HEAD_DIM = 64
V_DIM = 2 * HEAD_DIM
CONV_WIDTH = 31
CONV_HALO = 32

V7X_VMEM_BYTES = 64 * 1024 * 1024
VMEM_LIMIT_BYTES = V7X_VMEM_BYTES * 3 // 4

TM_PROJ = 512
TM_WIDE = 1024
PROJ_ROWS = 256
T_ATT = 1024
V_CHUNK = TM_WIDE
CONV_ROWS = 32
MIX_ROWS = 256
SUBLANES = 8
LANES = 128
NT_DIMS = (((1,), (1,)), ((), ()))
LOG2E = math.log2(math.e)
L_ROWS = 16


def _layer_norm(x, g, b):
    mu = jnp.mean(x, axis=-1, keepdims=True)
    xc = x - mu
    var = jnp.mean(xc * xc, axis=-1, keepdims=True)
    return xc * lax.rsqrt(var + LN_EPS) * g + b


def _sigmoid(x):
    return 1.0 / (1.0 + jnp.exp(-x))


def _dot(a, b):
    return jnp.dot(a, b, preferred_element_type=F32)


def _in_proj_kernel(x_ref, g_ref, b_ref, wglu_ref, wqT_ref, wk_ref, wvT_ref,
                    h_ref, u_ref, qT_ref, k_ref, vT_ref, *, apply_ln, q_scale):
    c = u_ref.shape[-1]
    tm = x_ref.shape[1]
    for r in range(0, tm, PROJ_ROWS):
        rows = slice(r, r + PROJ_ROWS)
        x = x_ref[0, rows, :]
        h = _layer_norm(x, g_ref[...], b_ref[...]) if apply_ln else x
        h_ref[0, rows, :] = h
        hb = h.astype(BF16)
        zg = _dot(hb, wglu_ref[...])
        u_ref[0, rows, :] = zg[:, :c] * _sigmoid(zg[:, c:])
        k_ref[0, rows, :] = _dot(hb, wk_ref[...]).astype(BF16)
        qT = lax.dot_general(wqT_ref[...], hb, NT_DIMS, preferred_element_type=F32)
        qT_ref[0, 0, :, rows] = (qT * q_scale).astype(BF16)
        vT = lax.dot_general(wvT_ref[...], hb, NT_DIMS, preferred_element_type=F32)
        vT_ref[0, 0, :, rows] = vT.astype(BF16)


def _in_proj(x, g, b, w_glu, w_qT, w_k, w_vT, *, apply_ln):
    B, S, D = x.shape
    tm = TM_WIDE
    c2 = w_glu.shape[1]
    nq = w_qT.shape[0]
    nv = w_vT.shape[0]
    const = lambda shape: pl.BlockSpec(shape, lambda bi, i: (0,) * len(shape))
    return pl.pallas_call(
        functools.partial(_in_proj_kernel, apply_ln=apply_ln, q_scale=HEAD_DIM ** -0.5 * LOG2E),
        grid=(B, S // tm),
        in_specs=[
            pl.BlockSpec((1, tm, D), lambda bi, i: (bi, i, 0)),
            const((1, D)), const((1, D)),
            const(w_glu.shape), const(w_qT.shape), const(w_k.shape), const(w_vT.shape),
        ],
        out_specs=[
            pl.BlockSpec((1, tm, D), lambda bi, i: (bi, i, 0)),
            pl.BlockSpec((1, tm, c2 // 2), lambda bi, i: (bi, i, 0)),
            pl.BlockSpec((1, 1, nq, V_CHUNK), lambda bi, i: (bi, i, 0, 0)),
            pl.BlockSpec((1, tm, nq), lambda bi, i: (bi, i, 0)),
            pl.BlockSpec((1, 1, nv, V_CHUNK), lambda bi, i: (bi, i, 0, 0)),
        ],
        out_shape=[
            jax.ShapeDtypeStruct((B, S, D), F32),
            jax.ShapeDtypeStruct((B, S, c2 // 2), F32),
            jax.ShapeDtypeStruct((B, S // V_CHUNK, nq, V_CHUNK), BF16),
            jax.ShapeDtypeStruct((B, S, nq), BF16),
            jax.ShapeDtypeStruct((B, S // V_CHUNK, nv, V_CHUNK), BF16),
        ],
        compiler_params=pltpu.CompilerParams(
            dimension_semantics=("arbitrary", "arbitrary"), vmem_limit_bytes=VMEM_LIMIT_BYTES),
        name="in_proj",
    )(x, g, b, w_glu, w_qT, w_k, w_vT)


PLAIN, DIAG = range(2)


def _attn_kernel(lq1_ref, lk1_ref, lq2_ref, lk2_ref, g_ref, qT_ref, k_ref, vT_ref, o_ref,
                 bcol_sc, ones_sc, tri_sc, s_sc, mcur_sc, m_sc, cnt_sc, step_sc, p_sc, alpha_sc, acc_sc,
                 *, slopes, lam_init, n_q):
    t = T_ATT
    hh = t // 2
    chunks = t // V_CHUNK
    h = pl.program_id(0)
    slope = jnp.float32(slopes[-1] * LOG2E)
    for head in range(len(slopes) - 1):
        slope = jnp.where(h == head, jnp.float32(slopes[head] * LOG2E), slope)

    @pl.when(pl.program_id(1) == 0)
    def _():
        kpos = lax.broadcasted_iota(jnp.int32, (t, LANES), 0).astype(F32)
        lane = lax.broadcasted_iota(jnp.int32, (t, LANES), 1)
        bias = slope * kpos
        hi = bias.astype(BF16).astype(F32)
        mid = (bias - hi).astype(BF16).astype(F32)
        lo = bias - hi - mid
        cols = jnp.where(lane == 0, hi, jnp.where(lane == 1, mid, jnp.where(lane == 2, lo, 0.0)))
        bcol_sc[...] = cols.astype(BF16)
        row = lax.broadcasted_iota(jnp.int32, ones_sc.shape, 0)
        ones_sc[...] = jnp.where(row < 3, 1.0, 0.0).astype(BF16)
        kk = lax.broadcasted_iota(jnp.int32, (hh, hh), 0)
        qq = lax.broadcasted_iota(jnp.int32, (hh, hh), 1)
        tri_sc[...] = jnp.where(kk <= qq, 0.0, NEG_INF)

    step_sc[...] = jnp.full(step_sc.shape, slope * t, F32)
    zeros_q = jnp.zeros((HEAD_DIM, t), BF16)
    ones_v = jnp.ones((L_ROWS, t), BF16)
    s1 = jnp.sum(lq1_ref[...] * lk1_ref[...], axis=-1, keepdims=True)
    s2 = jnp.sum(lq2_ref[...] * lk2_ref[...], axis=-1, keepdims=True)
    lam = jnp.exp(s1) - jnp.exp(s2) + lam_init
    g = g_ref[pl.ds(h, 1), :]

    def rhs_of(x):
        n = x.shape[1]
        top = jnp.concatenate([x[:HEAD_DIM], zeros_q[:, :n]], axis=1)
        bot = jnp.concatenate([zeros_q[:, :n], x[HEAD_DIM:]], axis=1)
        return jnp.concatenate([top, bot, ones_sc[:, :2 * n]], axis=0)

    def scores(i, j, kind):
        q = jnp.concatenate([qT_ref[0, i * chunks + c] for c in range(chunks)], axis=1)
        kb = k_ref[0, pl.ds(pl.multiple_of(j * t, t), t), :]
        lhs = jnp.concatenate([kb, bcol_sc[...]], axis=1)
        if kind == PLAIN:
            s = _dot(lhs, rhs_of(q))
            for c in range(2):
                blk = s[:, c * t:(c + 1) * t]
                s_sc[:, c * t:(c + 1) * t] = blk
                mcur_sc[:, c * t:(c + 1) * t] = jnp.max(blk, axis=0, keepdims=True)
            return
        s_hi = _dot(lhs, rhs_of(q[:, hh:]))
        s_lo = _dot(lhs[:hh], rhs_of(q[:, :hh]))
        tri = tri_sc[...]
        empty = jnp.full((hh, hh), NEG_INF, F32)
        for c in range(2):
            lo_lo = s_lo[:, c * hh:(c + 1) * hh] + tri
            lo_hi = s_hi[:hh, c * hh:(c + 1) * hh]
            hi_hi = s_hi[hh:, c * hh:(c + 1) * hh] + tri
            col = c * t
            s_sc[:hh, col:col + hh] = lo_lo
            s_sc[hh:, col:col + hh] = empty
            s_sc[:hh, col + hh:col + t] = lo_hi
            s_sc[hh:, col + hh:col + t] = hi_hi
            mcur_sc[:, col:col + hh] = jnp.max(lo_lo, axis=0, keepdims=True)
            mcur_sc[:, col + hh:col + t] = jnp.maximum(jnp.max(lo_hi, axis=0, keepdims=True),
                                                       jnp.max(hi_hi, axis=0, keepdims=True))

    def softmax(first):
        if first:
            cnt = jnp.zeros(cnt_sc.shape, F32)
            m_old = jnp.full(m_sc.shape, NEG_INF, F32)
        else:
            cnt = cnt_sc[...] + 1.0
            m_old = m_sc[...]
        cnt_sc[...] = cnt
        off = cnt * step_sc[...]
        m_new = jnp.maximum(m_old, mcur_sc[...] + off)
        alpha_sc[...] = jnp.exp2(m_old - m_new)
        p_sc[...] = jnp.exp2(s_sc[...] - (m_new - off)).astype(BF16)
        m_sc[...] = m_new

    def accumulate(j, kind):
        vT = [vT_ref[0, j * chunks + c] for c in range(chunks)]
        vT_aug = jnp.concatenate([jnp.concatenate(vT, axis=1), ones_v], axis=0)
        if kind == PLAIN:
            acc_sc[...] = alpha_sc[...] * acc_sc[...] + _dot(vT_aug, p_sc[...])
            return
        p_hi = jnp.concatenate([p_sc[:, hh:t], p_sc[:, t + hh:]], axis=1)
        p_lo = jnp.concatenate([p_sc[:hh, :hh], p_sc[:hh, t:t + hh]], axis=1)
        a_hi = _dot(vT_aug, p_hi)
        a_lo = _dot(vT_aug[:, :hh], p_lo)
        for c in range(2):
            lo = slice(c * t, c * t + hh)
            hi = slice(c * t + hh, (c + 1) * t)
            acc_sc[:, lo] = alpha_sc[:, lo] * acc_sc[:, lo] + a_lo[:, c * hh:(c + 1) * hh]
            acc_sc[:, hi] = alpha_sc[:, hi] * acc_sc[:, hi] + a_hi[:, c * hh:(c + 1) * hh]

    def finalize(i):
        acc = acc_sc[...]
        oT = (acc[:V_DIM, :t] / acc[V_DIM:V_DIM + 1, :t]
              - lam * (acc[:V_DIM, t:] / acc[V_DIM:V_DIM + 1, t:]))
        o = oT.T
        y = o * lax.rsqrt(jnp.mean(o * o, axis=-1, keepdims=True) + LN_EPS) * g
        o_ref[0, pl.ds(pl.multiple_of(i * t, t), t), :] = (y * (1.0 - lam_init)).astype(o_ref.dtype)

    def trip(j_acc, acc_kind, first, i_next, j_next, kind):
        j_acc, i_next, j_next = (jnp.asarray(x, jnp.int32) for x in (j_acc, i_next, j_next))
        accumulate(j_acc, acc_kind)
        softmax(first)
        scores(i_next, j_next, kind)

    assert n_q >= 3
    m_sc[...] = jnp.full(m_sc.shape, NEG_INF, F32)
    p_sc[...] = jnp.zeros(p_sc.shape, BF16)
    alpha_sc[...] = jnp.ones(alpha_sc.shape, F32)
    acc_sc[...] = jnp.zeros(acc_sc.shape, F32)
    scores(jnp.int32(0), jnp.int32(0), DIAG)
    trip(0, PLAIN, True, 1, 0, PLAIN)
    trip(0, DIAG, True, 1, 1, DIAG)
    finalize(jnp.int32(0))
    trip(0, PLAIN, False, 2, 0, PLAIN)

    def tile(i, carry):
        trip(i - 1, DIAG, True, i, 1, PLAIN)
        finalize(i - 1)

        def inner(j, c):
            trip(j - 1, PLAIN, False, i, j + 1, PLAIN)
            return c

        lax.fori_loop(1, i - 1, inner, 0)
        trip(i - 2, PLAIN, False, i, i, DIAG)
        trip(i - 1, PLAIN, False, jnp.minimum(i + 1, n_q - 1), 0, PLAIN)
        return carry

    lax.fori_loop(2, n_q, tile, 0)
    accumulate(jnp.int32(n_q - 1), DIAG)
    finalize(jnp.int32(n_q - 1))


def _attention(lq1, lk1, lq2, lk2, subln_g, qT, k, vT, *, lam_init):
    B, S, _ = k.shape
    H = N_HEADS
    t = T_ATT
    slopes = tuple(2.0 ** (-8.0 * i / H) for i in range(1, H + 1))
    lam_spec = pl.BlockSpec((1, HEAD_DIM), lambda h, b: (0, 0))
    return pl.pallas_call(
        functools.partial(_attn_kernel, slopes=slopes, lam_init=lam_init, n_q=S // t),
        grid=(H, B),
        in_specs=[
            lam_spec, lam_spec, lam_spec, lam_spec,
            pl.BlockSpec((H, V_DIM), lambda h, b: (0, 0)),
            pl.BlockSpec((1, S // V_CHUNK, 2 * HEAD_DIM, V_CHUNK), lambda h, b: (b, 0, h, 0)),
            pl.BlockSpec((1, S, 2 * HEAD_DIM), lambda h, b: (b, 0, h)),
            pl.BlockSpec((1, S // V_CHUNK, V_DIM, V_CHUNK), lambda h, b: (b, 0, h, 0)),
        ],
        out_specs=pl.BlockSpec((1, S, V_DIM), lambda h, b: (b, 0, h)),
        out_shape=jax.ShapeDtypeStruct((B, S, H * V_DIM), BF16),
        scratch_shapes=[
            pltpu.VMEM((t, LANES), BF16),
            pltpu.VMEM((LANES, 2 * t), BF16),
            pltpu.VMEM((t // 2, t // 2), F32),
            pltpu.VMEM((t, 2 * t), F32),
            pltpu.VMEM((1, 2 * t), F32),
            pltpu.VMEM((1, 2 * t), F32),
            pltpu.VMEM((1, 2 * t), F32),
            pltpu.VMEM((1, 2 * t), F32),
            pltpu.VMEM((t, 2 * t), BF16),
            pltpu.VMEM((1, 2 * t), F32),
            pltpu.VMEM((V_DIM + L_ROWS, 2 * t), F32),
        ],
        compiler_params=pltpu.CompilerParams(
            dimension_semantics=("arbitrary", "arbitrary"),
            vmem_limit_bytes=VMEM_LIMIT_BYTES),
        name="attn",
    )(lq1, lk1, lq2, lk2, subln_g, qT, k, vT)


def _mix_kernel(h_ref, u_ref, halo_ref, o_ref, cw_ref, cg_ref, cb_ref, wco_ref, wao_ref,
                wgate_ref, wo_ref, g1_ref, b1_ref, out_ref, win_sc, cwb_sc, conv_sc, *, alpha):
    i = pl.program_id(1)
    tm = u_ref.shape[1]
    d = h_ref.shape[-1]

    @pl.when(jnp.logical_and(pl.program_id(0) == 0, i == 0))
    def _():
        for w in range(CONV_WIDTH):
            cwb_sc[w] = jnp.broadcast_to(cw_ref[w:w + 1, :], cwb_sc.shape[1:])

    halo = halo_ref[0]
    win_sc[0, 0:CONV_HALO, :] = jnp.where(i == 0, jnp.zeros_like(halo), halo)
    win_sc[0, CONV_HALO:CONV_HALO + tm, :] = u_ref[0]
    for s in range(1, SUBLANES):
        win_sc[s, 0:CONV_HALO + tm - SUBLANES, :] = win_sc[0, s:s + CONV_HALO + tm - SUBLANES, :]
    first = CONV_HALO - (CONV_WIDTH - 1)
    blocks = CONV_ROWS // SUBLANES

    for r0 in range(0, tm, MIX_ROWS):
        for r in range(0, MIX_ROWS, CONV_ROWS):
            accs = [None] * blocks
            for w in range(CONV_WIDTH):
                a, s = divmod(first + w, SUBLANES)
                cw = cwb_sc[w]
                for rb in range(blocks):
                    row = r0 + r + (a + rb) * SUBLANES
                    term = win_sc[s, pl.ds(row, SUBLANES), :] * cw
                    accs[rb] = term if accs[rb] is None else accs[rb] + term
            for rb in range(blocks):
                conv_sc[r + rb * SUBLANES:r + (rb + 1) * SUBLANES, :] = accs[rb]
        rows = pl.ds(r0, MIX_ROWS)
        h = h_ref[0, rows, :]
        gates = _sigmoid(_dot(h.astype(BF16), wgate_ref[...]))
        y_attn = _dot(o_ref[0, rows, :], wao_ref[...])
        yc = _layer_norm(conv_sc[...], cg_ref[...], cb_ref[...])
        yc = yc * _sigmoid(yc)
        y_conv = _dot(yc.astype(BF16), wco_ref[...])
        merged = gates[:, :d] * y_conv + gates[:, d:] * y_attn
        r1 = alpha * h + _dot(merged.astype(BF16), wo_ref[...])
        out_ref[0, rows, :] = _layer_norm(r1, g1_ref[...], b1_ref[...])


def _mix(h, u, o, conv_w, conv_g, conv_b, w_co, w_ao, w_gate, w_o, g1, b1, *, alpha):
    B, S, D = h.shape
    C = u.shape[-1]
    tm = TM_WIDE
    halo_blocks = tm // CONV_HALO
    const = lambda shape: pl.BlockSpec(shape, lambda bi, i: (0,) * len(shape),
                                       pipeline_mode=pl.Buffered(1))
    return pl.pallas_call(
        functools.partial(_mix_kernel, alpha=alpha),
        grid=(B, S // tm),
        in_specs=[
            pl.BlockSpec((1, tm, D), lambda bi, i: (bi, i, 0)),
            pl.BlockSpec((1, tm, C), lambda bi, i: (bi, i, 0)),
            pl.BlockSpec((1, CONV_HALO, C), lambda bi, i: (bi, jnp.maximum(i * halo_blocks - 1, 0), 0)),
            pl.BlockSpec((1, tm, o.shape[-1]), lambda bi, i: (bi, i, 0)),
            const(conv_w.shape), const((1, C)), const((1, C)),
            const(w_co.shape), const(w_ao.shape), const(w_gate.shape), const(w_o.shape),
            const((1, D)), const((1, D)),
        ],
        out_specs=pl.BlockSpec((1, tm, D), lambda bi, i: (bi, i, 0)),
        out_shape=jax.ShapeDtypeStruct((B, S, D), F32),
        scratch_shapes=[
            pltpu.VMEM((SUBLANES, CONV_HALO + tm, C), F32),
            pltpu.VMEM((CONV_WIDTH, SUBLANES, C), F32),
            pltpu.VMEM((MIX_ROWS, C), F32),
        ],
        compiler_params=pltpu.CompilerParams(
            dimension_semantics=("arbitrary", "arbitrary"), vmem_limit_bytes=VMEM_LIMIT_BYTES),
        name="mix",
    )(h, u, u, o, conv_w, conv_g, conv_b, w_co, w_ao, w_gate, w_o, g1, b1)


def _ffn_kernel(h_ref, p_ref, w1_ref, w2_ref, wpg_ref, wple_ref, g2_ref, b2_ref, out_ref,
                *, alpha, ff_chunk):
    for r in range(0, h_ref.shape[1], PROJ_ROWS):
        rows = slice(r, r + PROJ_ROWS)
        h = h_ref[0, rows, :]
        hb = h.astype(BF16)
        acc = alpha * h
        for c in range(0, w1_ref.shape[1], ff_chunk):
            a = jnp.maximum(_dot(hb, w1_ref[:, c:c + ff_chunk]), 0.0)
            acc = acc + _dot((a * a).astype(BF16), w2_ref[c:c + ff_chunk, :])
        ple = _dot(p_ref[0, rows, :].astype(BF16), wple_ref[...])
        acc = acc + _sigmoid(_dot(hb, wpg_ref[...])) * ple
        out_ref[0, rows, :] = _layer_norm(acc, g2_ref[...], b2_ref[...])


def _ffn(h, p, w1, w2, w_pg, w_ple, g2, b2, *, alpha):
    B, S, D = h.shape
    tm = TM_WIDE
    const = lambda shape: pl.BlockSpec(shape, lambda bi, i: (0,) * len(shape),
                                       pipeline_mode=pl.Buffered(1))
    return pl.pallas_call(
        functools.partial(_ffn_kernel, alpha=alpha, ff_chunk=D),
        grid=(B, S // tm),
        in_specs=[
            pl.BlockSpec((1, tm, D), lambda bi, i: (bi, i, 0)),
            pl.BlockSpec((1, tm, p.shape[-1]), lambda bi, i: (bi, i, 0)),
            const(w1.shape), const(w2.shape), const(w_pg.shape), const(w_ple.shape),
            const((1, D)), const((1, D)),
        ],
        out_specs=pl.BlockSpec((1, tm, D), lambda bi, i: (bi, i, 0)),
        out_shape=jax.ShapeDtypeStruct((B, S, D), F32),
        compiler_params=pltpu.CompilerParams(
            dimension_semantics=("arbitrary", "arbitrary"), vmem_limit_bytes=VMEM_LIMIT_BYTES),
        name="ffn",
    )(h, p, w1, w2, w_pg, w_ple, g2, b2)


def kernel(x, p, ln0_g, ln0_b, w_in, conv_w, conv_ln_g, conv_ln_b, w_conv_out, lambda_q1, lambda_k1, lambda_q2, lambda_k2, subln_g, w_attn_out, w_o, ln1_g, ln1_b, w_ff1, w_ff2, w_ple, w_ple_gate, ln2_g, ln2_b):
    depth = w_in.shape[0]
    c_conv = conv_w.shape[-1]
    n_glu = 2 * c_conv
    n_qk = N_HEADS * 2 * HEAD_DIM
    n_v = N_HEADS * V_DIM
    alpha = (2.0 * depth) ** 0.25
    row = lambda v: v.reshape(1, -1).astype(F32)

    h = x
    for i in range(depth):
        lam_init = 0.8 - 0.6 * math.exp(-0.3 * i)
        w = w_in[i].astype(BF16)
        w_glu = w[:, :n_glu]
        w_qT = w[:, n_glu:n_glu + n_qk].T
        w_k = w[:, n_glu + n_qk:n_glu + 2 * n_qk]
        w_vT = w[:, n_glu + 2 * n_qk:n_glu + 2 * n_qk + n_v].T
        w_gate = w[:, n_glu + 2 * n_qk + n_v:]

        h, u, qT, k, vT = _in_proj(h, row(ln0_g), row(ln0_b), w_glu, w_qT, w_k, w_vT,
                                   apply_ln=(i == 0))
        o = _attention(row(lambda_q1[i]), row(lambda_k1[i]), row(lambda_q2[i]), row(lambda_k2[i]),
                       subln_g[i].reshape(N_HEADS, V_DIM).astype(F32), qT, k, vT, lam_init=lam_init)
        h = _mix(h, u, o, conv_w[i].astype(F32), row(conv_ln_g[i]), row(conv_ln_b[i]),
                 w_conv_out[i].astype(BF16), w_attn_out[i].astype(BF16), w_gate,
                 w_o[i].astype(BF16), row(ln1_g[i]), row(ln1_b[i]), alpha=alpha)
        h = _ffn(h, p[i], w_ff1[i].astype(BF16), w_ff2[i].astype(BF16), w_ple_gate[i].astype(BF16),
                 w_ple[i].astype(BF16), row(ln2_g[i]), row(ln2_b[i]), alpha=alpha)
    return h
```

```python
import functools
import math

import jax
import jax.numpy as jnp
from jax import lax
from jax.experimental import pallas as pl
from jax.experimental.pallas import tpu as pltpu

F32 = jnp.float32
BF16 = jnp.bfloat16

LN_EPS = 1e-5
NEG_INF = -1e30
N_HEADS = 4
HEAD_DIM = 64
V_DIM = 2 * HEAD_DIM
CONV_WIDTH = 31
CONV_HALO = 32

V7X_VMEM_BYTES = 64 * 1024 * 1024
VMEM_LIMIT_BYTES = V7X_VMEM_BYTES * 3 // 4

TM_PROJ = 512
TM_WIDE = 1024
PROJ_ROWS = 256
T_ATT = 1024
V_CHUNK = TM_WIDE
CONV_ROWS = 32
MIX_ROWS = 256
PIN_ROWS = 96
SUBLANES = 8
LANES = 128
NT_DIMS = (((1,), (1,)), ((), ()))
LOG2E = math.log2(math.e)
L_ROWS = 16


def _layer_norm(x, g, b):
    mu = jnp.mean(x, axis=-1, keepdims=True)
    xc = x - mu
    var = jnp.mean(xc * xc, axis=-1, keepdims=True)
    return xc * lax.rsqrt(var + LN_EPS) * g + b


def _sigmoid(x):
    return 1.0 / (1.0 + jnp.exp(-x))


def _dot(a, b):
    return jnp.dot(a, b, preferred_element_type=F32)


def _in_proj_kernel(x_ref, g_ref, b_ref, wglu_ref, wqT_ref, wk_ref, wvT_ref,
                    h_ref, u_ref, qT_ref, k_ref, vT_ref, *, apply_ln, q_scale):
    c = u_ref.shape[-1]
    tm = x_ref.shape[1]
    for r in range(0, tm, PROJ_ROWS):
        rows = slice(r, r + PROJ_ROWS)
        x = x_ref[0, rows, :]
        h = _layer_norm(x, g_ref[...], b_ref[...]) if apply_ln else x
        h_ref[0, rows, :] = h
        hb = h.astype(BF16)
        zg = _dot(hb, wglu_ref[...])
        u_ref[0, rows, :] = zg[:, :c] * _sigmoid(zg[:, c:])
        k_ref[0, rows, :] = _dot(hb, wk_ref[...]).astype(BF16)
        qT = lax.dot_general(wqT_ref[...], hb, NT_DIMS, preferred_element_type=F32)
        qT_ref[0, 0, :, rows] = (qT * q_scale).astype(BF16)
        vT = lax.dot_general(wvT_ref[...], hb, NT_DIMS, preferred_element_type=F32)
        vT_ref[0, 0, :, rows] = vT.astype(BF16)


def _in_proj(x, g, b, w_glu, w_qT, w_k, w_vT, *, apply_ln):
    B, S, D = x.shape
    tm = TM_WIDE
    c2 = w_glu.shape[1]
    nq = w_qT.shape[0]
    nv = w_vT.shape[0]
    const = lambda shape: pl.BlockSpec(shape, lambda bi, i: (0,) * len(shape))
    return pl.pallas_call(
        functools.partial(_in_proj_kernel, apply_ln=apply_ln, q_scale=HEAD_DIM ** -0.5 * LOG2E),
        grid=(B, S // tm),
        in_specs=[
            pl.BlockSpec((1, tm, D), lambda bi, i: (bi, i, 0)),
            const((1, D)), const((1, D)),
            const(w_glu.shape), const(w_qT.shape), const(w_k.shape), const(w_vT.shape),
        ],
        out_specs=[
            pl.BlockSpec((1, tm, D), lambda bi, i: (bi, i, 0)),
            pl.BlockSpec((1, tm, c2 // 2), lambda bi, i: (bi, i, 0)),
            pl.BlockSpec((1, 1, nq, V_CHUNK), lambda bi, i: (bi, i, 0, 0)),
            pl.BlockSpec((1, tm, nq), lambda bi, i: (bi, i, 0)),
            pl.BlockSpec((1, 1, nv, V_CHUNK), lambda bi, i: (bi, i, 0, 0)),
        ],
        out_shape=[
            jax.ShapeDtypeStruct((B, S, D), F32),
            jax.ShapeDtypeStruct((B, S, c2 // 2), F32),
            jax.ShapeDtypeStruct((B, S // V_CHUNK, nq, V_CHUNK), BF16),
            jax.ShapeDtypeStruct((B, S, nq), BF16),
            jax.ShapeDtypeStruct((B, S // V_CHUNK, nv, V_CHUNK), BF16),
        ],
        compiler_params=pltpu.CompilerParams(
            dimension_semantics=("arbitrary", "arbitrary"), vmem_limit_bytes=VMEM_LIMIT_BYTES),
        name="in_proj",
    )(x, g, b, w_glu, w_qT, w_k, w_vT)


PLAIN, DIAG = range(2)


def _attn_kernel(lq1_ref, lk1_ref, lq2_ref, lk2_ref, g_ref, qT_ref, k_ref, vT_ref, o_ref,
                 bcol_sc, ones_sc, tri_sc, s_sc, mcur_sc, m_sc, cnt_sc, step_sc, p_sc, alpha_sc, acc_sc,
                 *, slopes, lam_init, n_q):
    t = T_ATT
    hh = t // 2
    chunks = t // V_CHUNK
    h = pl.program_id(0)
    slope = jnp.float32(slopes[-1] * LOG2E)
    for head in range(len(slopes) - 1):
        slope = jnp.where(h == head, jnp.float32(slopes[head] * LOG2E), slope)

    @pl.when(pl.program_id(1) == 0)
    def _():
        kpos = lax.broadcasted_iota(jnp.int32, (t, LANES), 0).astype(F32)
        lane = lax.broadcasted_iota(jnp.int32, (t, LANES), 1)
        bias = slope * kpos
        hi = bias.astype(BF16).astype(F32)
        mid = (bias - hi).astype(BF16).astype(F32)
        lo = bias - hi - mid
        cols = jnp.where(lane == 0, hi, jnp.where(lane == 1, mid, jnp.where(lane == 2, lo, 0.0)))
        bcol_sc[...] = cols.astype(BF16)
        row = lax.broadcasted_iota(jnp.int32, ones_sc.shape, 0)
        ones_sc[...] = jnp.where(row < 3, 1.0, 0.0).astype(BF16)
        kk = lax.broadcasted_iota(jnp.int32, (hh, hh), 0)
        qq = lax.broadcasted_iota(jnp.int32, (hh, hh), 1)
        tri_sc[...] = jnp.where(kk <= qq, 0.0, NEG_INF)

    step_sc[...] = jnp.full(step_sc.shape, slope * t, F32)
    zeros_q = jnp.zeros((HEAD_DIM, t), BF16)
    ones_v = jnp.ones((L_ROWS, t), BF16)
    s1 = jnp.sum(lq1_ref[...] * lk1_ref[...], axis=-1, keepdims=True)
    s2 = jnp.sum(lq2_ref[...] * lk2_ref[...], axis=-1, keepdims=True)
    lam = jnp.exp(s1) - jnp.exp(s2) + lam_init
    g = g_ref[pl.ds(h, 1), :]

    def rhs_of(x):
        n = x.shape[1]
        top = jnp.concatenate([x[:HEAD_DIM], zeros_q[:, :n]], axis=1)
        bot = jnp.concatenate([zeros_q[:, :n], x[HEAD_DIM:]], axis=1)
        return jnp.concatenate([top, bot, ones_sc[:, :2 * n]], axis=0)

    def scores(i, j, kind):
        q = jnp.concatenate([qT_ref[0, i * chunks + c] for c in range(chunks)], axis=1)
        kb = k_ref[0, pl.ds(pl.multiple_of(j * t, t), t), :]
        lhs = jnp.concatenate([kb, bcol_sc[...]], axis=1)
        if kind == PLAIN:
            s = _dot(lhs, rhs_of(q))
            for c in range(2):
                blk = s[:, c * t:(c + 1) * t]
                s_sc[:, c * t:(c + 1) * t] = blk
                mcur_sc[:, c * t:(c + 1) * t] = jnp.max(blk, axis=0, keepdims=True)
            return
        s_hi = _dot(lhs, rhs_of(q[:, hh:]))
        s_lo = _dot(lhs[:hh], rhs_of(q[:, :hh]))
        tri = tri_sc[...]
        empty = jnp.full((hh, hh), NEG_INF, F32)
        for c in range(2):
            lo_lo = s_lo[:, c * hh:(c + 1) * hh] + tri
            lo_hi = s_hi[:hh, c * hh:(c + 1) * hh]
            hi_hi = s_hi[hh:, c * hh:(c + 1) * hh] + tri
            col = c * t
            s_sc[:hh, col:col + hh] = lo_lo
            s_sc[hh:, col:col + hh] = empty
            s_sc[:hh, col + hh:col + t] = lo_hi
            s_sc[hh:, col + hh:col + t] = hi_hi
            mcur_sc[:, col:col + hh] = jnp.max(lo_lo, axis=0, keepdims=True)
            mcur_sc[:, col + hh:col + t] = jnp.maximum(jnp.max(lo_hi, axis=0, keepdims=True),
                                                       jnp.max(hi_hi, axis=0, keepdims=True))

    def softmax(first):
        if first:
            cnt = jnp.zeros(cnt_sc.shape, F32)
            m_old = jnp.full(m_sc.shape, NEG_INF, F32)
        else:
            cnt = cnt_sc[...] + 1.0
            m_old = m_sc[...]
        cnt_sc[...] = cnt
        off = cnt * step_sc[...]
        m_new = jnp.maximum(m_old, mcur_sc[...] + off)
        alpha_sc[...] = jnp.exp2(m_old - m_new)
        p_sc[...] = jnp.exp2(s_sc[...] - (m_new - off)).astype(BF16)
        m_sc[...] = m_new

    def accumulate(j, kind):
        vT = [vT_ref[0, j * chunks + c] for c in range(chunks)]
        vT_aug = jnp.concatenate([jnp.concatenate(vT, axis=1), ones_v], axis=0)
        if kind == PLAIN:
            acc_sc[...] = alpha_sc[...] * acc_sc[...] + _dot(vT_aug, p_sc[...])
            return
        p_hi = jnp.concatenate([p_sc[:, hh:t], p_sc[:, t + hh:]], axis=1)
        p_lo = jnp.concatenate([p_sc[:hh, :hh], p_sc[:hh, t:t + hh]], axis=1)
        a_hi = _dot(vT_aug, p_hi)
        a_lo = _dot(vT_aug[:, :hh], p_lo)
        for c in range(2):
            lo = slice(c * t, c * t + hh)
            hi = slice(c * t + hh, (c + 1) * t)
            acc_sc[:, lo] = alpha_sc[:, lo] * acc_sc[:, lo] + a_lo[:, c * hh:(c + 1) * hh]
            acc_sc[:, hi] = alpha_sc[:, hi] * acc_sc[:, hi] + a_hi[:, c * hh:(c + 1) * hh]

    def finalize(i):
        acc = acc_sc[...]
        oT = (acc[:V_DIM, :t] / acc[V_DIM:V_DIM + 1, :t]
              - lam * (acc[:V_DIM, t:] / acc[V_DIM:V_DIM + 1, t:]))
        o = oT.T
        y = o * lax.rsqrt(jnp.mean(o * o, axis=-1, keepdims=True) + LN_EPS) * g
        o_ref[0, pl.ds(pl.multiple_of(i * t, t), t), :] = (y * (1.0 - lam_init)).astype(o_ref.dtype)

    def trip(j_acc, acc_kind, first, i_next, j_next, kind):
        j_acc, i_next, j_next = (jnp.asarray(x, jnp.int32) for x in (j_acc, i_next, j_next))
        accumulate(j_acc, acc_kind)
        softmax(first)
        scores(i_next, j_next, kind)

    assert n_q >= 3
    m_sc[...] = jnp.full(m_sc.shape, NEG_INF, F32)
    p_sc[...] = jnp.zeros(p_sc.shape, BF16)
    alpha_sc[...] = jnp.ones(alpha_sc.shape, F32)
    acc_sc[...] = jnp.zeros(acc_sc.shape, F32)
    scores(jnp.int32(0), jnp.int32(0), DIAG)
    trip(0, PLAIN, True, 1, 0, PLAIN)
    trip(0, DIAG, True, 1, 1, DIAG)
    finalize(jnp.int32(0))
    trip(0, PLAIN, False, 2, 0, PLAIN)

    def tile(i, carry):
        trip(i - 1, DIAG, True, i, 1, PLAIN)
        finalize(i - 1)

        def inner(j, c):
            trip(j - 1, PLAIN, False, i, j + 1, PLAIN)
            return c

        lax.fori_loop(1, i - 1, inner, 0)
        trip(i - 2, PLAIN, False, i, i, DIAG)
        trip(i - 1, PLAIN, False, jnp.minimum(i + 1, n_q - 1), 0, PLAIN)
        return carry

    lax.fori_loop(2, n_q, tile, 0)
    accumulate(jnp.int32(n_q - 1), DIAG)
    finalize(jnp.int32(n_q - 1))


def _attention(lq1, lk1, lq2, lk2, subln_g, qT, k, vT, *, lam_init):
    B, S, _ = k.shape
    H = N_HEADS
    t = T_ATT
    slopes = tuple(2.0 ** (-8.0 * i / H) for i in range(1, H + 1))
    lam_spec = pl.BlockSpec((1, HEAD_DIM), lambda h, b: (0, 0))
    return pl.pallas_call(
        functools.partial(_attn_kernel, slopes=slopes, lam_init=lam_init, n_q=S // t),
        grid=(H, B),
        in_specs=[
            lam_spec, lam_spec, lam_spec, lam_spec,
            pl.BlockSpec((H, V_DIM), lambda h, b: (0, 0)),
            pl.BlockSpec((1, S // V_CHUNK, 2 * HEAD_DIM, V_CHUNK), lambda h, b: (b, 0, h, 0)),
            pl.BlockSpec((1, S, 2 * HEAD_DIM), lambda h, b: (b, 0, h)),
            pl.BlockSpec((1, S // V_CHUNK, V_DIM, V_CHUNK), lambda h, b: (b, 0, h, 0)),
        ],
        out_specs=pl.BlockSpec((1, S, V_DIM), lambda h, b: (b, 0, h)),
        out_shape=jax.ShapeDtypeStruct((B, S, H * V_DIM), BF16),
        scratch_shapes=[
            pltpu.VMEM((t, LANES), BF16),
            pltpu.VMEM((LANES, 2 * t), BF16),
            pltpu.VMEM((t // 2, t // 2), F32),
            pltpu.VMEM((t, 2 * t), F32),
            pltpu.VMEM((1, 2 * t), F32),
            pltpu.VMEM((1, 2 * t), F32),
            pltpu.VMEM((1, 2 * t), F32),
            pltpu.VMEM((1, 2 * t), F32),
            pltpu.VMEM((t, 2 * t), BF16),
            pltpu.VMEM((1, 2 * t), F32),
            pltpu.VMEM((V_DIM + L_ROWS, 2 * t), F32),
        ],
        compiler_params=pltpu.CompilerParams(
            dimension_semantics=("arbitrary", "arbitrary"),
            vmem_limit_bytes=VMEM_LIMIT_BYTES),
        name="attn",
    )(lq1, lk1, lq2, lk2, subln_g, qT, k, vT)


def _mix_kernel(h_ref, u_ref, halo_ref, o_ref, cw_ref, cg_ref, cb_ref, wco_ref, wao_ref,
                wgate_ref, wo_ref, g1_ref, b1_ref, out_ref, win_sc, cwb_sc, conv_sc, *, alpha):
    i = pl.program_id(1)
    tm = u_ref.shape[1]
    d = h_ref.shape[-1]

    @pl.when(jnp.logical_and(pl.program_id(0) == 0, i == 0))
    def _():
        for w in range(CONV_WIDTH):
            cwb_sc[w] = jnp.broadcast_to(cw_ref[w:w + 1, :], cwb_sc.shape[1:])

    halo = halo_ref[0]
    win_sc[0, 0:CONV_HALO, :] = jnp.where(i == 0, jnp.zeros_like(halo), halo)
    win_sc[0, CONV_HALO:CONV_HALO + tm, :] = u_ref[0]
    for s in range(1, SUBLANES):
        win_sc[s, 0:CONV_HALO + tm - SUBLANES, :] = win_sc[0, s:s + CONV_HALO + tm - SUBLANES, :]
    first = CONV_HALO - (CONV_WIDTH - 1)
    blocks = CONV_ROWS // SUBLANES

    for r0 in range(0, tm, MIX_ROWS):
        for r in range(0, MIX_ROWS, CONV_ROWS):
            accs = [None] * blocks
            for w in range(CONV_WIDTH):
                a, s = divmod(first + w, SUBLANES)
                cw = cwb_sc[w]
                for rb in range(blocks):
                    row = r0 + r + (a + rb) * SUBLANES
                    term = win_sc[s, pl.ds(row, SUBLANES), :] * cw
                    accs[rb] = term if accs[rb] is None else accs[rb] + term
            for rb in range(blocks):
                conv_sc[r + rb * SUBLANES:r + (rb + 1) * SUBLANES, :] = accs[rb]
            if r == PIN_ROWS:
                bits = lax.bitcast_convert_type(accs[0], jnp.uint32)
                pin = lax.bitcast_convert_type(
                    lax.shift_right_logical(lax.shift_right_logical(bits, jnp.uint32(16)), jnp.uint32(16)), F32)
        rows = pl.ds(r0, MIX_ROWS)
        h = h_ref[0, rows, :]
        c = pin.shape[1]
        h_gate = jnp.concatenate(
            [jnp.concatenate([h[:SUBLANES, :c] + pin, h[:SUBLANES, c:]], axis=1), h[SUBLANES:]], axis=0)
        gates = _sigmoid(_dot(h_gate.astype(BF16), wgate_ref[...]))
        y_attn = _dot(o_ref[0, rows, :], wao_ref[...])
        yc = _layer_norm(conv_sc[...], cg_ref[...], cb_ref[...])
        yc = yc * _sigmoid(yc)
        y_conv = _dot(yc.astype(BF16), wco_ref[...])
        merged = gates[:, :d] * y_conv + gates[:, d:] * y_attn
        r1 = alpha * h + _dot(merged.astype(BF16), wo_ref[...])
        out_ref[0, rows, :] = _layer_norm(r1, g1_ref[...], b1_ref[...])


def _mix(h, u, o, conv_w, conv_g, conv_b, w_co, w_ao, w_gate, w_o, g1, b1, *, alpha):
    B, S, D = h.shape
    C = u.shape[-1]
    tm = TM_PROJ
    halo_blocks = tm // CONV_HALO
    const = lambda shape: pl.BlockSpec(shape, lambda bi, i: (0,) * len(shape))
    return pl.pallas_call(
        functools.partial(_mix_kernel, alpha=alpha),
        grid=(B, S // tm),
        in_specs=[
            pl.BlockSpec((1, tm, D), lambda bi, i: (bi, i, 0)),
            pl.BlockSpec((1, tm, C), lambda bi, i: (bi, i, 0)),
            pl.BlockSpec((1, CONV_HALO, C), lambda bi, i: (bi, jnp.maximum(i * halo_blocks - 1, 0), 0)),
            pl.BlockSpec((1, tm, o.shape[-1]), lambda bi, i: (bi, i, 0)),
            const(conv_w.shape), const((1, C)), const((1, C)),
            const(w_co.shape), const(w_ao.shape), const(w_gate.shape), const(w_o.shape),
            const((1, D)), const((1, D)),
        ],
        out_specs=pl.BlockSpec((1, tm, D), lambda bi, i: (bi, i, 0)),
        out_shape=jax.ShapeDtypeStruct((B, S, D), F32),
        scratch_shapes=[
            pltpu.VMEM((SUBLANES, CONV_HALO + tm, C), F32),
            pltpu.VMEM((CONV_WIDTH, SUBLANES, C), F32),
            pltpu.VMEM((MIX_ROWS, C), F32),
        ],
        compiler_params=pltpu.CompilerParams(
            dimension_semantics=("arbitrary", "arbitrary"), vmem_limit_bytes=VMEM_LIMIT_BYTES),
        name="mix",
    )(h, u, u, o, conv_w, conv_g, conv_b, w_co, w_ao, w_gate, w_o, g1, b1)


def _ffn_kernel(h_ref, p_ref, w1_ref, w2_ref, wpg_ref, wple_ref, g2_ref, b2_ref, out_ref,
                *, alpha, ff_chunk):
    for r in range(0, h_ref.shape[1], PROJ_ROWS):
        rows = slice(r, r + PROJ_ROWS)
        h = h_ref[0, rows, :]
        hb = h.astype(BF16)
        acc = alpha * h
        for c in range(0, w1_ref.shape[1], ff_chunk):
            a = jnp.maximum(_dot(hb, w1_ref[:, c:c + ff_chunk]), 0.0)
            acc = acc + _dot((a * a).astype(BF16), w2_ref[c:c + ff_chunk, :])
        ple = _dot(p_ref[0, rows, :].astype(BF16), wple_ref[...])
        acc = acc + _sigmoid(_dot(hb, wpg_ref[...])) * ple
        out_ref[0, rows, :] = _layer_norm(acc, g2_ref[...], b2_ref[...])


def _ffn(h, p, w1, w2, w_pg, w_ple, g2, b2, *, alpha):
    B, S, D = h.shape
    tm = TM_WIDE
    const = lambda shape: pl.BlockSpec(shape, lambda bi, i: (0,) * len(shape),
                                       pipeline_mode=pl.Buffered(1))
    return pl.pallas_call(
        functools.partial(_ffn_kernel, alpha=alpha, ff_chunk=D),
        grid=(B, S // tm),
        in_specs=[
            pl.BlockSpec((1, tm, D), lambda bi, i: (bi, i, 0)),
            pl.BlockSpec((1, tm, p.shape[-1]), lambda bi, i: (bi, i, 0)),
            const(w1.shape), const(w2.shape), const(w_pg.shape), const(w_ple.shape),
            const((1, D)), const((1, D)),
        ],
        out_specs=pl.BlockSpec((1, tm, D), lambda bi, i: (bi, i, 0)),
        out_shape=jax.ShapeDtypeStruct((B, S, D), F32),
        compiler_params=pltpu.CompilerParams(
            dimension_semantics=("arbitrary", "arbitrary"), vmem_limit_bytes=VMEM_LIMIT_BYTES),
        name="ffn",
    )(h, p, w1, w2, w_pg, w_ple, g2, b2)


def kernel(x, p, ln0_g, ln0_b, w_in, conv_w, conv_ln_g, conv_ln_b, w_conv_out, lambda_q1, lambda_k1, lambda_q2, lambda_k2, subln_g, w_attn_out, w_o, ln1_g, ln1_b, w_ff1, w_ff2, w_ple, w_ple_gate, ln2_g, ln2_b):
    depth = w_in.shape[0]
    c_conv = conv_w.shape[-1]
    n_glu = 2 * c_conv
    n_qk = N_HEADS * 2 * HEAD_DIM
    n_v = N_HEADS * V_DIM
    alpha = (2.0 * depth) ** 0.25
    row = lambda v: v.reshape(1, -1).astype(F32)

    h = x
    for i in range(depth):
        lam_init = 0.8 - 0.6 * math.exp(-0.3 * i)
        w = w_in[i].astype(BF16)
        w_glu = w[:, :n_glu]
        w_qT = w[:, n_glu:n_glu + n_qk].T
        w_k = w[:, n_glu + n_qk:n_glu + 2 * n_qk]
        w_vT = w[:, n_glu + 2 * n_qk:n_glu + 2 * n_qk + n_v].T
        w_gate = w[:, n_glu + 2 * n_qk + n_v:]

        h, u, qT, k, vT = _in_proj(h, row(ln0_g), row(ln0_b), w_glu, w_qT, w_k, w_vT,
                                   apply_ln=(i == 0))
        o = _attention(row(lambda_q1[i]), row(lambda_k1[i]), row(lambda_q2[i]), row(lambda_k2[i]),
                       subln_g[i].reshape(N_HEADS, V_DIM).astype(F32), qT, k, vT, lam_init=lam_init)
        h = _mix(h, u, o, conv_w[i].astype(F32), row(conv_ln_g[i]), row(conv_ln_b[i]),
                 w_conv_out[i].astype(BF16), w_attn_out[i].astype(BF16), w_gate,
                 w_o[i].astype(BF16), row(ln1_g[i]), row(ln1_b[i]), alpha=alpha)
        h = _ffn(h, p[i], w_ff1[i].astype(BF16), w_ff2[i].astype(BF16), w_ple_gate[i].astype(BF16),
                 w_ple[i].astype(BF16), row(ln2_g[i]), row(ln2_b[i]), alpha=alpha)
    return h
```
